```python
import math
import jax
import jax.numpy as jnp
from jax import lax
import numpy as np

D_MODEL = 1024
BATCH = 4
SEQ = 4096
DEPTH = 4
DEC_BATCH = 128
DEC_SEQ = 4
PAST_LEN = 2048
PAGE_SIZE = 128

D_MIX = D_MODEL
A_HEADS = 4
A_HD = 64
A_W = A_HEADS * 2 * A_HD
B_HEADS = 4
B_HD = 64
B_W = B_HEADS * B_HD
NSA_BLOCK = 64
NSA_TOPK = 16
WINDOW = 512
C_W = D_MIX - A_W - B_W
CONV_W = 3
X_HEADS = 4
X_HD = 64
X_W = X_HEADS * X_HD
N_MEM = 256
D_FF = -(-(8 * D_MODEL) // (3 * 256)) * 256
ROPE_THETA = 500000.0
ROPE_FRACTION = 4
EPS = 1e-6
NEG_INF = -1e30
FORCE_SCORE = 1e4
Q_BLOCK = 128
IN_SIZES = (A_W, A_W, A_W, B_W, 2 * B_HD, 2 * B_HD, 2 * B_HD, 3 * B_HEADS, C_W, C_W, C_W)
IN_COLS = sum(IN_SIZES)
IN_SPLITS = tuple(sum(IN_SIZES[:i + 1]) for i in range(len(IN_SIZES) - 1))
NORM_MIX_PRE = 0
NORM_MIX_POST = 1
NORM_X_PRE = 2
NORM_X_POST = 3
NORM_FFN_PRE = 4
NORM_FFN_POST = 5
NORM_MEM = 6
N_NORMS = 7

kernel_name = 'hybrid_diff_nsa_conv_decoder_step'


def rms_norm(x, w):
    xf = x.astype(jnp.float32)
    y = xf * lax.rsqrt(jnp.mean(xf * xf, axis=-1, keepdims=True) + EPS)
    return (y * w.astype(jnp.float32)).astype(x.dtype)


def rope_partial(x, pos):
    rd = x.shape[-1] // ROPE_FRACTION
    half = rd // 2
    inv = ROPE_THETA ** (-jnp.arange(half, dtype=jnp.float32) / half)
    ang = pos.astype(jnp.float32)[:, None] * inv[None, :]
    cos = jnp.cos(ang)[:, None, :]
    sin = jnp.sin(ang)[:, None, :]
    xr = x[..., :rd].astype(jnp.float32)
    x1, x2 = xr[..., :half], xr[..., half:]
    rot = jnp.concatenate([x1 * cos - x2 * sin, x2 * cos + x1 * sin], axis=-1).astype(x.dtype)
    return jnp.concatenate([rot, x[..., rd:]], axis=-1)


def rope_kv(kv, pos):
    k = rope_partial(kv[..., :B_HD][:, :, None, :], pos)[:, :, 0, :]
    return jnp.concatenate([k, kv[..., B_HD:]], axis=-1)


def group_inputs(h, pos, w_in):
    Bn, S, _ = h.shape
    aq, ak, av, bq, bkc, bks, bkw, bg, cx, cb, cc = jnp.split(h @ w_in, IN_SPLITS, axis=-1)
    q_a = rope_partial(aq.reshape(Bn, S, 2 * A_HEADS, A_HD), pos).reshape(Bn, S, A_HEADS, 2, A_HD)
    k_a = rope_partial(ak.reshape(Bn, S, 2 * A_HEADS, A_HD), pos).reshape(Bn, S, A_HEADS, 2 * A_HD)
    v_a = av.reshape(Bn, S, A_HEADS, 2 * A_HD)
    q_b = bq.reshape(Bn, S, B_HEADS, B_HD)
    g_b = bg.reshape(Bn, S, B_HEADS, 3)
    return q_a, k_a, v_a, q_b, g_b, bkc, rope_kv(bks, pos), rope_kv(bkw, pos), cc * cx, cb


def diff_lambda_value(lw, layer):
    lam_init = 0.8 - 0.6 * math.exp(-0.3 * layer)
    lw = lw.astype(jnp.float32)
    lam = jnp.exp(jnp.sum(lw[0] * lw[1])) - jnp.exp(jnp.sum(lw[2] * lw[3])) + lam_init
    return lam, lam_init


def diff_attn(q, k, v, q_pos, k_pos, lam, lam_init, subln_w):
    s = jnp.einsum('bqhmd,bthmd->bhmqt', q, k).astype(jnp.float32) * (A_HD ** -0.5)
    causal = k_pos[None, :] <= q_pos[:, None]
    p = jax.nn.softmax(jnp.where(causal, s, NEG_INF), axis=-1)
    a = p[:, :, 0] - lam * p[:, :, 1]
    o = jnp.einsum('bhqt,bthe->bqhe', a.astype(v.dtype), v)
    return rms_norm(o, subln_w) * (1.0 - lam_init)


def nsa_compress(kv, pe, w_cmp):
    Bn, T, _ = kv.shape
    blocks = kv.reshape(Bn, T // NSA_BLOCK, NSA_BLOCK, 2, B_HD) + jnp.swapaxes(pe, 0, 1)
    m = jnp.mean(blocks, axis=2)
    return m[:, :, 0] @ w_cmp[0], m[:, :, 1] @ w_cmp[1]


def nsa_attend(q, g, q_pos, kc, vc, slc_blocks, kv_win, kw_pos):
    scale = B_HD ** -0.5
    Bn = q.shape[0]
    nb = kc.shape[1]
    blk = jnp.arange(nb, dtype=jnp.int32)
    q_r = rope_partial(q, q_pos)
    valid = (blk[None, :] + 1) * NSA_BLOCK <= q_pos[:, None] + 1
    sc = jnp.einsum('bqhd,bnd->bqhn', q, kc).astype(jnp.float32) * scale
    pc = jax.nn.softmax(jnp.where(valid[None, :, None, :], sc, NEG_INF), axis=-1)
    pc = pc * jnp.any(valid, axis=-1)[None, :, None, None].astype(jnp.float32)
    o_c = jnp.einsum('bqhn,bnd->bqhd', pc.astype(vc.dtype), vc)
    cur = (q_pos // NSA_BLOCK)[:, None] == blk[None, :]
    imp = jnp.sum(pc, axis=2)
    sel = jnp.where(cur[None], FORCE_SCORE, jnp.where(valid[None], imp, NEG_INF))
    top_s, idx = lax.top_k(sel, min(NSA_TOPK, nb))
    picked = slc_blocks[jnp.arange(Bn)[:, None, None], idx]
    ks, vs = picked[..., :B_HD], picked[..., B_HD:]
    tok = idx[..., None] * NSA_BLOCK + jnp.arange(NSA_BLOCK, dtype=jnp.int32)
    smask = (top_s > NEG_INF / 2)[..., None] & (tok <= q_pos[None, :, None, None])
    ss = jnp.einsum('bqhd,bqkld->bqhkl', q_r, ks).astype(jnp.float32) * scale
    ss = jnp.where(smask[:, :, None], ss, NEG_INF)
    ps = jax.nn.softmax(ss.reshape(ss.shape[:3] + (-1,)), axis=-1).reshape(ss.shape)
    o_s = jnp.einsum('bqhkl,bqkld->bqhd', ps.astype(vs.dtype), vs)
    kw, vw = kv_win[..., :B_HD], kv_win[..., B_HD:]
    wmask = (kw_pos[None, :] <= q_pos[:, None]) & (kw_pos[None, :] > q_pos[:, None] - WINDOW) & (kw_pos[None, :] >= 0)
    sw = jnp.einsum('bqhd,btd->bqht', q_r, kw).astype(jnp.float32) * scale
    pw = jax.nn.softmax(jnp.where(wmask[None, :, None, :], sw, NEG_INF), axis=-1)
    o_w = jnp.einsum('bqht,btd->bqhd', pw.astype(vw.dtype), vw)
    gate = jax.nn.sigmoid(g.astype(jnp.float32)).astype(q.dtype)
    return gate[..., 0:1] * o_c + gate[..., 1:2] * o_s + gate[..., 2:3] * o_w


def conv_mixer(c_all, b_gate, conv_w):
    S = b_gate.shape[1]
    z = conv_w[0] * c_all[:, 0:S]
    for j in range(1, CONV_W):
        z = z + conv_w[j] * c_all[:, j:j + S]
    return b_gate * z


def mem_kv(mem, norm_mem, w_kv):
    m = rms_norm(mem, norm_mem)
    kv = (m @ w_kv).reshape(mem.shape[0], mem.shape[1], 2, X_HEADS, X_HD)
    return kv[:, :, 0], kv[:, :, 1]


def cross_attn(h, w_q, mk, mv, w_o):
    Bn, S, _ = h.shape
    q = (h @ w_q).reshape(Bn, S, X_HEADS, X_HD)
    s = jnp.einsum('bshd,bmhd->bhsm', q, mk).astype(jnp.float32) * (X_HD ** -0.5)
    p = jax.nn.softmax(s, axis=-1)
    o = jnp.einsum('bhsm,bmhd->bshd', p.astype(mv.dtype), mv).reshape(Bn, S, X_W)
    return o @ w_o


def swiglu(h, w_up, w_down):
    g, u = jnp.split(h @ w_up, 2, axis=-1)
    return (jax.nn.silu(g) * u) @ w_down


def residual_tail(x, mixed, w_out, nw, mk, mv, w_q_mem, w_o_mem, w_ffn_up, w_ffn_down):
    x = x + rms_norm(mixed @ w_out, nw[NORM_MIX_POST])
    h = rms_norm(x, nw[NORM_X_PRE])
    x = x + rms_norm(cross_attn(h, w_q_mem, mk, mv, w_o_mem), nw[NORM_X_POST])
    h = rms_norm(x, nw[NORM_FFN_PRE])
    return x + rms_norm(swiglu(h, w_ffn_up, w_ffn_down), nw[NORM_FFN_POST])


def prompt_mixer(h, layer, w_in, diff_lambda, diff_subln, nsa_pe, nsa_w_cmp, conv_w):
    Bn, S, _ = h.shape
    nqb = S // Q_BLOCK
    pos = jnp.arange(S, dtype=jnp.int32)
    pos_blocks = pos.reshape(nqb, Q_BLOCK)
    q_a, k_a, v_a, q_b, g_b, kv_cmp, kv_slc, kv_win, c_in, b_gate = group_inputs(h, pos, w_in)

    def to_blocks(t):
        return jnp.swapaxes(t.reshape((Bn, nqb, Q_BLOCK) + t.shape[2:]), 0, 1)

    def from_blocks(t, width):
        return jnp.swapaxes(t, 0, 1).reshape(Bn, S, width)

    lam, lam_init = diff_lambda_value(diff_lambda, layer)
    k_a4 = k_a.reshape(Bn, S, A_HEADS, 2, A_HD)
    o_a = lax.map(lambda a: diff_attn(a[0], k_a4, v_a, a[1], pos, lam, lam_init, diff_subln),
                  (to_blocks(q_a), pos_blocks))
    o_a = from_blocks(o_a, A_W)
    kc, vc = nsa_compress(kv_cmp, nsa_pe, nsa_w_cmp)
    slc_blocks = kv_slc.reshape(Bn, S // NSA_BLOCK, NSA_BLOCK, 2 * B_HD)
    win_pad = jnp.pad(kv_win, ((0, 0), (WINDOW, 0), (0, 0)))

    def nsa_block(a):
        q_i, g_i, p_i, i = a
        kv_i = lax.dynamic_slice_in_dim(win_pad, i * Q_BLOCK, Q_BLOCK + WINDOW, axis=1)
        kpos_i = i * Q_BLOCK - WINDOW + jnp.arange(Q_BLOCK + WINDOW, dtype=jnp.int32)
        return nsa_attend(q_i, g_i, p_i, kc, vc, slc_blocks, kv_i, kpos_i)

    o_b = lax.map(nsa_block, (to_blocks(q_b), to_blocks(g_b), pos_blocks, jnp.arange(nqb, dtype=jnp.int32)))
    o_b = from_blocks(o_b, B_W)
    hist = jnp.zeros((Bn, CONV_W - 1, C_W), c_in.dtype)
    o_c = conv_mixer(jnp.concatenate([hist, c_in], axis=1), b_gate, conv_w)
    mixed = jnp.concatenate([o_a, o_b, o_c], axis=-1)
    win_state = jnp.pad(kv_win, ((0, 0), (max(0, WINDOW - S), 0), (0, 0)))[:, -WINDOW:]
    conv_state = c_in[:, S - (CONV_W - 1):]
    return mixed, (k_a, v_a, kv_cmp, kv_slc, win_state, conv_state)


def sample_mixer(h, layer, cache_diff_k, cache_diff_v, cache_nsa_cmp, cache_nsa_slc, state_nsa_win, state_conv,
                 page_table, w_in, diff_lambda, diff_subln, nsa_pe, nsa_w_cmp, conv_w):
    Bn, S, _ = h.shape
    past = page_table.shape[1] * cache_diff_k.shape[2]
    T = past + S
    pos = past + jnp.arange(S, dtype=jnp.int32)
    q_a, k_a, v_a, q_b, g_b, kv_cmp, kv_slc, kv_win, c_in, b_gate = group_inputs(h, pos, w_in)

    def gather(cache):
        rows = cache[layer, page_table]
        return rows.reshape((Bn, past) + cache.shape[3:])

    lam, lam_init = diff_lambda_value(diff_lambda, layer)
    k_all = jnp.concatenate([gather(cache_diff_k), k_a], axis=1).reshape(Bn, T, A_HEADS, 2, A_HD)
    v_all = jnp.concatenate([gather(cache_diff_v), v_a], axis=1)
    o_a = diff_attn(q_a, k_all, v_all, pos, jnp.arange(T, dtype=jnp.int32), lam, lam_init, diff_subln)
    o_a = o_a.reshape(Bn, S, A_W)
    nb = -(-T // NSA_BLOCK)
    pad = nb * NSA_BLOCK - T

    def pad_rows(t):
        return jnp.pad(t, ((0, 0), (0, pad), (0, 0)))

    kc, vc = nsa_compress(pad_rows(jnp.concatenate([gather(cache_nsa_cmp), kv_cmp], axis=1)), nsa_pe, nsa_w_cmp)
    slc_blocks = pad_rows(jnp.concatenate([gather(cache_nsa_slc), kv_slc], axis=1)).reshape(Bn, nb, NSA_BLOCK, 2 * B_HD)
    wb = state_nsa_win.shape[2]
    win_all = jnp.concatenate([state_nsa_win[layer], kv_win], axis=1)
    kpos_w = past - wb + jnp.arange(wb + S, dtype=jnp.int32)
    o_b = nsa_attend(q_b, g_b, pos, kc, vc, slc_blocks, win_all, kpos_w).reshape(Bn, S, B_W)
    c_all = jnp.concatenate([state_conv[layer], c_in], axis=1)
    o_c = conv_mixer(c_all, b_gate, conv_w)
    mixed = jnp.concatenate([o_a, o_b, o_c], axis=-1)
    return mixed, (k_a, v_a, kv_cmp, kv_slc, win_all[:, S:], c_all[:, S:])


def setup_inputs(seed: int = 0) -> dict:
    key = jax.random.key(seed)
    ks = jax.random.split(key, 24)
    n_pages = PAST_LEN // PAGE_SIZE
    n_used = DEC_BATCH * n_pages
    n_pool = n_used + max(1, n_used // 4)
    win_buf = min(WINDOW, PAST_LEN)

    def nrm(k, shape, scale=1.0):
        return jax.random.normal(k, shape, jnp.float32) * scale

    page_table = jax.random.permutation(ks[0], n_pool)[:n_used].reshape(DEC_BATCH, n_pages).astype(jnp.int32)
    return {
        'x_prompt': nrm(ks[1], (BATCH, SEQ, D_MODEL)),
        'x_sample': nrm(ks[2], (DEC_BATCH, DEC_SEQ, D_MODEL)),
        'cache_diff_k': nrm(ks[3], (DEPTH, n_pool, PAGE_SIZE, A_HEADS, 2 * A_HD)),
        'cache_diff_v': nrm(ks[4], (DEPTH, n_pool, PAGE_SIZE, A_HEADS, 2 * A_HD)),
        'cache_nsa_cmp': nrm(ks[5], (DEPTH, n_pool, PAGE_SIZE, 2 * B_HD)),
        'cache_nsa_slc': nrm(ks[6], (DEPTH, n_pool, PAGE_SIZE, 2 * B_HD)),
        'state_nsa_win': nrm(ks[7], (DEPTH, DEC_BATCH, win_buf, 2 * B_HD)),
        'state_conv': nrm(ks[8], (DEPTH, DEC_BATCH, CONV_W - 1, C_W)),
        'cache_mem_k': nrm(ks[9], (DEPTH, DEC_BATCH, N_MEM, X_HEADS, X_HD)),
        'cache_mem_v': nrm(ks[10], (DEPTH, DEC_BATCH, N_MEM, X_HEADS, X_HD)),
        'page_table': page_table,
        'mem_prompt': nrm(ks[11], (BATCH, N_MEM, D_MODEL)),
        'norm_w': 1.0 + nrm(ks[12], (DEPTH, N_NORMS, D_MODEL), 0.02),
        'w_in': nrm(ks[13], (DEPTH, D_MODEL, IN_COLS), D_MODEL ** -0.5),
        'w_out': nrm(ks[14], (DEPTH, D_MIX, D_MODEL), D_MIX ** -0.5),
        'diff_lambda': nrm(ks[15], (DEPTH, 4, A_HD), 0.1),
        'diff_subln': 1.0 + nrm(ks[16], (DEPTH, 2 * A_HD), 0.02),
        'nsa_pe': nrm(ks[17], (DEPTH, 2, NSA_BLOCK, B_HD), 0.1),
        'nsa_w_cmp': nrm(ks[18], (DEPTH, 2, B_HD, B_HD), B_HD ** -0.5),
        'conv_w': nrm(ks[19], (DEPTH, CONV_W, C_W), CONV_W ** -0.5),
        'w_q_mem': nrm(ks[20], (DEPTH, D_MODEL, X_W), D_MODEL ** -0.5),
        'w_kv_mem': nrm(ks[21], (DEPTH, D_MODEL, 2 * X_W), D_MODEL ** -0.5),
        'w_o_mem': nrm(ks[22], (DEPTH, X_W, D_MODEL), X_W ** -0.5),
        'w_ffn_up': nrm(ks[23], (DEPTH, D_MODEL, 2 * D_FF), D_MODEL ** -0.5),
        'w_ffn_down': nrm(jax.random.fold_in(ks[23], 1), (DEPTH, D_FF, D_MODEL), D_FF ** -0.5),
    }


def reference(x_prompt, x_sample, cache_diff_k, cache_diff_v, cache_nsa_cmp, cache_nsa_slc, state_nsa_win,
              state_conv, cache_mem_k, cache_mem_v, page_table, mem_prompt, norm_w, w_in, w_out, diff_lambda,
              diff_subln, nsa_pe, nsa_w_cmp, conv_w, w_q_mem, w_kv_mem, w_o_mem, w_ffn_up, w_ffn_down):
    xp = x_prompt
    xs = x_sample
    prompt_rows = []
    sample_rows = []
    for l in range(DEPTH):
        nw = norm_w[l]
        hp = rms_norm(xp, nw[NORM_MIX_PRE])
        mixed_p, st_p = prompt_mixer(hp, l, w_in[l], diff_lambda[l], diff_subln[l], nsa_pe[l], nsa_w_cmp[l], conv_w[l])
        mk_p, mv_p = mem_kv(mem_prompt, nw[NORM_MEM], w_kv_mem[l])
        xp = residual_tail(xp, mixed_p, w_out[l], nw, mk_p, mv_p, w_q_mem[l], w_o_mem[l], w_ffn_up[l], w_ffn_down[l])
        prompt_rows.append(st_p + (mk_p, mv_p))
        hs = rms_norm(xs, nw[NORM_MIX_PRE])
        mixed_s, st_s = sample_mixer(hs, l, cache_diff_k, cache_diff_v, cache_nsa_cmp, cache_nsa_slc, state_nsa_win,
                                     state_conv, page_table, w_in[l], diff_lambda[l], diff_subln[l], nsa_pe[l],
                                     nsa_w_cmp[l], conv_w[l])
        xs = residual_tail(xs, mixed_s, w_out[l], nw, cache_mem_k[l], cache_mem_v[l], w_q_mem[l], w_o_mem[l],
                           w_ffn_up[l], w_ffn_down[l])
        sample_rows.append(st_s)
    p_diff_k, p_diff_v, p_nsa_cmp, p_nsa_slc, p_nsa_win, p_conv, p_mem_k, p_mem_v = [
        jnp.stack(t) for t in zip(*prompt_rows)]
    s_diff_k, s_diff_v, s_nsa_cmp, s_nsa_slc, s_nsa_win, s_conv = [jnp.stack(t) for t in zip(*sample_rows)]
    y_prompt = xp
    y_sample = xs
    return (y_prompt, y_sample, p_diff_k, p_diff_v, p_nsa_cmp, p_nsa_slc, p_nsa_win, p_conv, p_mem_k, p_mem_v,
            s_diff_k, s_diff_v, s_nsa_cmp, s_nsa_slc, s_nsa_win, s_conv)
```

```python
import functools
import math

import jax
import jax.numpy as jnp
from jax import lax
from jax.experimental import pallas as pl
from jax.experimental.pallas import tpu as pltpu

F32 = jnp.float32
BF16 = jnp.bfloat16
I32 = jnp.int32

A_HEADS = 4
A_HD = 64
A_W = A_HEADS * 2 * A_HD
B_HEADS = 4
B_HD = 64
B_W = B_HEADS * B_HD
NSA_BLOCK = 64
NSA_TOPK = 16
WINDOW = 512
C_W = 256
CONV_W = 3
X_HEADS = 4
X_HD = 64
X_W = X_HEADS * X_HD
ROPE_THETA = 500000.0
ROPE_HALF = (A_HD // 4) // 2
EPS = 1e-6
NEG_INF = -1e30
FORCE_SCORE = 1e4
Q_SCALE = A_HD ** -0.5
LANE = 128
IN_COLS_PADDED = 3072
NEW_ROWS = 16
TOK_PAD = 8
VMEM_LIMIT = 56 * 1024 * 1024

NORM_MIX_PRE, NORM_MIX_POST, NORM_X_PRE, NORM_X_POST, NORM_FFN_PRE, NORM_FFN_POST, NORM_MEM = range(7)

NT_DIMS = (((1,), (1,)), ((), ()))


def _params(*sem):
    return pltpu.CompilerParams(dimension_semantics=sem, vmem_limit_bytes=VMEM_LIMIT)


def _pick_tile(n, target, mult=16):
    best = None
    for t in range(mult, min(n, target) + 1, mult):
        if n % t == 0:
            best = t
    assert best is not None, (n, target)
    return best


def _rms(x, w):
    return x * lax.rsqrt(jnp.mean(x * x, axis=-1, keepdims=True) + EPS) * w


def _dot(a, b):
    return jnp.dot(a, b, preferred_element_type=F32)


def _dot_nt(a, b):
    return lax.dot_general(a, b, NT_DIMS, preferred_element_type=F32)


def _rope(z, c, s1, s2):
    outs = []
    for j in range(z.shape[1] // LANE):
        t = z[:, j * LANE:(j + 1) * LANE]
        outs.append(t * c + pltpu.roll(t, LANE - ROPE_HALF, 1) * s1 + pltpu.roll(t, ROPE_HALF, 1) * s2)
    return outs[0] if len(outs) == 1 else jnp.concatenate(outs, axis=1)


def _softmax_update(s, v, m, l, acc):
    m_new = jnp.maximum(m, jnp.max(s, axis=-1, keepdims=True))
    p = jnp.exp(s - m_new)
    alpha = jnp.exp(m - m_new)
    l = alpha * l + jnp.sum(p, axis=-1, keepdims=True)
    acc = alpha * acc + _dot(p.astype(BF16), v)
    return m_new, l, acc


def _proj_in_kernel(x_ref, nw_ref, w_ref, rope_ref, qa_ref, ka_ref, kab_ref, va_ref, vab_ref, qbp_ref, qbrp_ref,
                    kvc_ref, kvs_ref, kvsb_ref, kvw_ref, kvwb_ref, g_ref, cin_ref, bgate_ref):
    h = _rms(x_ref[...], nw_ref[...]).astype(BF16)
    c = rope_ref[:, 0:LANE]
    s1 = rope_ref[:, LANE:2 * LANE]
    s2 = rope_ref[:, 2 * LANE:3 * LANE]

    def mm(a, b):
        return _dot(h, w_ref[:, a:b])

    qa_ref[...] = (_rope(mm(0, 512), c, s1, s2) * Q_SCALE).astype(BF16)
    zk = _rope(mm(512, 1024), c, s1, s2)
    ka_ref[...] = zk
    kab_ref[...] = zk.astype(BF16)
    zv = mm(1024, 1536)
    va_ref[...] = zv
    vab_ref[...] = zv.astype(BF16)

    zb = mm(1536, 1792) * Q_SCALE
    zbr = _rope(zb, c, s1, s2)
    first = lax.broadcasted_iota(I32, (zb.shape[0], LANE), 1) < B_HD

    def pad_heads(z):
        outs = []
        for j in range(2):
            t = z[:, j * LANE:(j + 1) * LANE]
            outs.append(jnp.where(first, t, 0.0))
            outs.append(jnp.where(first, pltpu.roll(t, B_HD, 1), 0.0))
        return jnp.concatenate(outs, axis=1)

    qbp_ref[...] = pad_heads(zb).astype(BF16)
    qbrp_ref[...] = pad_heads(zbr).astype(BF16)

    zkv = mm(1792, 2176)
    kvc_ref[...] = zkv[:, 0:LANE]
    ckv = jnp.where(first, c, 1.0)
    s1kv = jnp.where(first, s1, 0.0)
    s2kv = jnp.where(first, s2, 0.0)
    ks = _rope(zkv[:, LANE:2 * LANE], ckv, s1kv, s2kv)
    kvs_ref[...] = ks
    kvsb_ref[...] = ks.astype(BF16)
    kw = _rope(zkv[:, 2 * LANE:3 * LANE], ckv, s1kv, s2kv)
    kvw_ref[...] = kw
    kvwb_ref[...] = kw.astype(BF16)

    zc = mm(2176, 2944)
    cin_ref[...] = zc[:, 2 * C_W:3 * C_W] * zc[:, 0:C_W]
    bgate_ref[...] = zc[:, C_W:2 * C_W]
    g_ref[...] = mm(2944, 3072)


def _proj_in(x, nw, w, rope, tm):
    n, d = x.shape
    row = lambda wdt: pl.BlockSpec((tm, wdt), lambda i: (i, 0))
    widths = [(512, BF16), (512, F32), (512, BF16), (512, F32), (512, BF16), (512, BF16), (512, BF16),
              (128, F32), (128, F32), (128, BF16), (128, F32), (128, BF16), (128, F32), (C_W, F32), (C_W, F32)]
    return pl.pallas_call(
        _proj_in_kernel,
        grid=(n // tm,),
        in_specs=[row(d), pl.BlockSpec((1, d), lambda i: (0, 0)),
                  pl.BlockSpec((d, IN_COLS_PADDED), lambda i: (0, 0)), row(3 * LANE)],
        out_specs=[row(wd) for wd, _ in widths],
        out_shape=[jax.ShapeDtypeStruct((n, wd), dt) for wd, dt in widths],
        compiler_params=_params("parallel"),
        name="proj_in",
    )(x, nw, w, rope)


def _diff_lambda(lw):
    a = jnp.sum(lw[0:1] * lw[1:2], axis=-1, keepdims=True)
    b = jnp.sum(lw[2:3] * lw[3:4], axis=-1, keepdims=True)
    return jnp.exp(a) - jnp.exp(b)


def _diff_finish(o1, o2, lam, subw, one_minus_init):
    d = o1 - lam * o2
    return _rms(d, subw) * one_minus_init


def _diff_prompt_kernel(lw_ref, q_ref, k_ref, v_ref, subw_ref, o_ref, *, tq, lam_init):
    qi = pl.program_id(2)
    q = q_ref[...]
    lane = lax.broadcasted_iota(I32, (tq, LANE), 1)
    zero = jnp.zeros_like(q)
    qs = jnp.concatenate([jnp.where(lane < A_HD, q, zero), jnp.where(lane >= A_HD, q, zero)], axis=0)
    rows = 2 * tq

    def step(j, carry, diag):
        m, l, acc = carry
        start = pl.multiple_of(j * tq, tq)
        s = _dot_nt(qs, k_ref[pl.ds(start, tq), :])
        if diag:
            r = lax.broadcasted_iota(I32, (rows, tq), 0) & (tq - 1)
            cidx = lax.broadcasted_iota(I32, (rows, tq), 1)
            s = jnp.where(cidx <= r, s, NEG_INF)
        return _softmax_update(s, v_ref[pl.ds(start, tq), :], m, l, acc)

    init = (jnp.full((rows, 1), NEG_INF, F32), jnp.zeros((rows, 1), F32), jnp.zeros((rows, LANE), F32))
    carry = lax.fori_loop(0, qi, lambda j, cr: step(j, cr, False), init)
    m, l, acc = step(qi, carry, True)
    o = acc / l
    lam = _diff_lambda(lw_ref[...]) + lam_init
    o_ref[...] = _diff_finish(o[:tq], o[tq:], lam, subw_ref[...], 1.0 - lam_init).astype(BF16)


def _diff_prompt(qa, kab, vab, lw, subw, lam_init, nbatch, seq, tq):
    nq = seq // tq
    return pl.pallas_call(
        functools.partial(_diff_prompt_kernel, tq=tq, lam_init=lam_init),
        grid=(nbatch, A_HEADS, nq),
        in_specs=[pl.BlockSpec((4, A_HD), lambda b, h, i: (0, 0)),
                  pl.BlockSpec((tq, LANE), lambda b, h, i: (b * nq + i, h)),
                  pl.BlockSpec((seq, LANE), lambda b, h, i: (b, h)),
                  pl.BlockSpec((seq, LANE), lambda b, h, i: (b, h)),
                  pl.BlockSpec((1, LANE), lambda b, h, i: (0, 0))],
        out_specs=pl.BlockSpec((tq, LANE), lambda b, h, i: (b * nq + i, h)),
        out_shape=jax.ShapeDtypeStruct((nbatch * seq, A_W), BF16),
        compiler_params=_params("parallel", "parallel", "arbitrary"),
        name="diff_prompt",
    )(lw, qa, kab, vab, subw)


def _diff_sample_kernel(pt_ref, lw_ref, subw_ref, q_ref, kn_ref, vn_ref, *rest, n_pages, n_new, lam_init):
    del pt_ref
    k_pages = rest[:n_pages]
    v_pages = rest[n_pages:2 * n_pages]
    o_ref = rest[2 * n_pages]
    q = q_ref[...]
    lane = lax.broadcasted_iota(I32, q.shape, 1)
    zero = jnp.zeros_like(q)
    qs = jnp.concatenate(
        [jnp.where((lane >= hm * A_HD) & (lane < (hm + 1) * A_HD), q, zero) for hm in range(2 * A_HEADS)], axis=0)
    rows = qs.shape[0]
    s_past = [_dot_nt(qs, kp[...].astype(BF16)) for kp in k_pages]
    s_new = _dot_nt(qs, kn_ref[...].astype(BF16))
    tok = lax.broadcasted_iota(I32, s_new.shape, 0) & (TOK_PAD - 1)
    jn = lax.broadcasted_iota(I32, s_new.shape, 1)
    s_new = jnp.where((jn <= tok) & (jn < n_new), s_new, NEG_INF)
    m = jnp.max(s_new, axis=-1, keepdims=True)
    for s in s_past:
        m = jnp.maximum(m, jnp.max(s, axis=-1, keepdims=True))
    p_new = jnp.exp(s_new - m)
    l = jnp.sum(p_new, axis=-1, keepdims=True)
    acc = _dot(p_new.astype(BF16), vn_ref[...].astype(BF16))
    for s, vp in zip(s_past, v_pages):
        p = jnp.exp(s - m)
        l = l + jnp.sum(p, axis=-1, keepdims=True)
        acc = acc + _dot(p.astype(BF16), vp[...].astype(BF16))
    o = acc / l
    lam = _diff_lambda(lw_ref[...]) + lam_init
    outs = []
    for h in range(A_HEADS):
        o1 = o[(2 * h) * TOK_PAD:(2 * h + 1) * TOK_PAD, h * LANE:(h + 1) * LANE]
        o2 = o[(2 * h + 1) * TOK_PAD:(2 * h + 2) * TOK_PAD, h * LANE:(h + 1) * LANE]
        outs.append(_diff_finish(o1, o2, lam, subw_ref[...], 1.0 - lam_init))
    o_ref[...] = jnp.concatenate(outs, axis=1).astype(BF16)


def _diff_sample(pt_flat, lw, subw, q8, kn, vn, cache_k, cache_v, layer, lam_init, n_pages, n_new):
    nseq = q8.shape[0]
    page_rows = cache_k.shape[2]

    def page_spec(p):
        return pl.BlockSpec((None, None, page_rows, A_W), lambda b, pt, p=p: (layer, pt[b * n_pages + p], 0, 0))

    seq_spec = lambda r, w: pl.BlockSpec((None, r, w), lambda b, pt: (b, 0, 0))
    grid_spec = pltpu.PrefetchScalarGridSpec(
        num_scalar_prefetch=1,
        grid=(nseq,),
        in_specs=[pl.BlockSpec((4, A_HD), lambda b, pt: (0, 0)), pl.BlockSpec((1, LANE), lambda b, pt: (0, 0)),
                  seq_spec(TOK_PAD, A_W), seq_spec(NEW_ROWS, A_W), seq_spec(NEW_ROWS, A_W)]
        + [page_spec(p) for p in range(n_pages)] + [page_spec(p) for p in range(n_pages)],
        out_specs=seq_spec(TOK_PAD, A_W),
    )
    return pl.pallas_call(
        functools.partial(_diff_sample_kernel, n_pages=n_pages, n_new=n_new, lam_init=lam_init),
        grid_spec=grid_spec,
        out_shape=jax.ShapeDtypeStruct((nseq, TOK_PAD, A_W), BF16),
        compiler_params=_params("parallel"),
        name="diff_sample",
    )(pt_flat, lw, subw, q8, kn, vn, *([cache_k] * n_pages), *([cache_v] * n_pages))


def _nsa_compress_kernel(kv_ref, pe_ref, w_ref, o_ref, *, nb):
    kv = kv_ref[...]
    m = jnp.sum(kv.reshape(nb, NSA_BLOCK, LANE), axis=1) * (1.0 / NSA_BLOCK)
    m = m + jnp.mean(pe_ref[...], axis=0, keepdims=True)
    o_ref[...] = jnp.dot(m, w_ref[...], preferred_element_type=F32, precision=lax.Precision.HIGHEST)


def _nsa_compress(kvc, pe_cat, w_bd, nbatch, seq):
    nb = seq // NSA_BLOCK
    return pl.pallas_call(
        functools.partial(_nsa_compress_kernel, nb=nb),
        grid=(nbatch,),
        in_specs=[pl.BlockSpec((seq, LANE), lambda b: (b, 0)), pl.BlockSpec((NSA_BLOCK, LANE), lambda b: (0, 0)),
                  pl.BlockSpec((LANE, LANE), lambda b: (0, 0))],
        out_specs=pl.BlockSpec((None, nb, LANE), lambda b: (b, 0, 0)),
        out_shape=jax.ShapeDtypeStruct((nbatch, nb, LANE), F32),
        compiler_params=_params("parallel"),
        name="nsa_compress",
    )(kvc, pe_cat, w_bd)


def _nsa_compressed_branch(qc, kcvc, pos0, tq):
    nb = kcvc.shape[0]
    rows = B_HEADS * tq
    lane = lax.broadcasted_iota(I32, kcvc.shape, 1)
    kc = jnp.where(lane < B_HD, kcvc, 0.0).astype(BF16)
    vc = jnp.where(lane >= B_HD, kcvc, 0.0).astype(BF16)
    sc = _dot_nt(qc, kc)
    qpos = pos0 + (lax.broadcasted_iota(I32, (rows, nb), 0) & (tq - 1))
    blk = lax.broadcasted_iota(I32, (rows, nb), 1)
    valid = (blk + 1) * NSA_BLOCK <= qpos + 1
    scm = jnp.where(valid, sc, NEG_INF)
    e = jnp.exp(scm - jnp.max(scm, axis=-1, keepdims=True))
    pc = e / jnp.sum(e, axis=-1, keepdims=True)
    pc = pc * jnp.where(qpos + 1 >= NSA_BLOCK, 1.0, 0.0)
    o_c = _dot(pc.astype(BF16), vc)
    imp = jnp.sum(pc.reshape(B_HEADS, tq, nb), axis=0)
    qpos_t = pos0 + lax.broadcasted_iota(I32, (tq, nb), 0)
    blk_t = lax.broadcasted_iota(I32, (tq, nb), 1)
    cur = (qpos_t // NSA_BLOCK) == blk_t
    valid_t = (blk_t + 1) * NSA_BLOCK <= qpos_t + 1
    sel = jnp.where(cur, FORCE_SCORE, jnp.where(valid_t, imp, NEG_INF))
    rank = jnp.zeros((tq, nb), I32)
    for i in range(nb):
        col = sel[:, i:i + 1]
        ahead = (col > sel) | ((col == sel) & (blk_t > i))
        rank = rank + jnp.where(ahead, 1, 0)
    chosen = (rank < NSA_TOPK) & (sel > NEG_INF / 2)
    return o_c, jnp.where(chosen, 1.0, 0.0).astype(BF16)


def _expand_blocks(chosen, first_tok, width):
    nb = chosen.shape[1]
    tok = first_tok + lax.broadcasted_iota(I32, (nb, width), 1)
    blk = lax.broadcasted_iota(I32, (nb, width), 0)
    e = jnp.where((tok // NSA_BLOCK) == blk, 1.0, 0.0).astype(BF16)
    return _dot(chosen, e)


def _nsa_combine(g, o_c, o_s, o_w, tq):
    gate = jax.nn.sigmoid(g)
    lane = lax.broadcasted_iota(I32, (tq, LANE), 1)
    comb = []
    for h in range(B_HEADS):
        sl = slice(h * tq, (h + 1) * tq)
        comb.append(gate[:, 3 * h:3 * h + 1] * o_c[sl] + gate[:, 3 * h + 1:3 * h + 2] * o_s[sl]
                    + gate[:, 3 * h + 2:3 * h + 3] * o_w[sl])
    groups = [jnp.where(lane < B_HD, pltpu.roll(comb[2 * j], B_HD, 1), comb[2 * j + 1]) for j in range(2)]
    return jnp.concatenate(groups, axis=1)


def _nsa_prompt_kernel(qbp_ref, qbrp_ref, kcvc_ref, kvs_ref, kvw_ref, g_ref, o_ref, *, tq):
    qi = pl.program_id(1)
    pos0 = qi * tq
    rows = B_HEADS * tq
    qc = jnp.concatenate([qbp_ref[:, h * LANE:(h + 1) * LANE] for h in range(B_HEADS)], axis=0)
    o_c, chosen = _nsa_compressed_branch(qc, kcvc_ref[...], pos0, tq)
    qr = jnp.concatenate([qbrp_ref[:, h * LANE:(h + 1) * LANE] for h in range(B_HEADS)], axis=0)

    r_t = lax.broadcasted_iota(I32, (tq, tq), 0)
    c_t = lax.broadcasted_iota(I32, (tq, tq), 1)

    def masked(s, ok):
        return jnp.where(ok[None], s.reshape(B_HEADS, tq, tq), NEG_INF).reshape(rows, tq)

    init = (jnp.full((rows, 1), NEG_INF, F32), jnp.zeros((rows, 1), F32), jnp.zeros((rows, LANE), F32))

    def sel_step(j, carry, diag):
        start = pl.multiple_of(j * tq, tq)
        kv = kvs_ref[pl.ds(start, tq), :]
        ok = _expand_blocks(chosen, start, tq) > 0.5
        if diag:
            ok = ok & (c_t <= r_t)
        return _softmax_update(masked(_dot_nt(qr, kv), ok), kv, *carry)

    carry = lax.fori_loop(0, qi, lambda j, cr: sel_step(j, cr, False), init)
    m, l, acc = sel_step(qi, carry, True)
    o_s = acc / l

    nw = WINDOW // tq

    def win_step(j, carry, ok):
        start = pl.multiple_of(j * tq, tq)
        kv = kvw_ref[pl.ds(start, tq), :]
        s = _dot_nt(qr, kv)
        if ok is not None:
            s = masked(s, ok)
        return _softmax_update(s, kv, *carry)

    carry = win_step(jnp.maximum(qi - nw, 0), init, (c_t > r_t) & (qi >= nw))
    carry = lax.fori_loop(jnp.maximum(qi - nw + 1, 0), qi, lambda j, cr: win_step(j, cr, None), carry)
    m, l, acc = win_step(qi, carry, c_t <= r_t)
    o_w = acc / l

    o_ref[...] = _nsa_combine(g_ref[...], o_c, o_s, o_w, tq).astype(BF16)


def _nsa_prompt(qbp, qbrp, kcvc, kvsb, kvwb, g, nbatch, seq, tq):
    nq = seq // tq
    nb = seq // NSA_BLOCK
    tile = lambda w: pl.BlockSpec((tq, w), lambda b, i: (b * nq + i, 0))
    return pl.pallas_call(
        functools.partial(_nsa_prompt_kernel, tq=tq),
        grid=(nbatch, nq),
        in_specs=[tile(4 * LANE), tile(4 * LANE), pl.BlockSpec((None, nb, LANE), lambda b, i: (b, 0, 0)),
                  pl.BlockSpec((seq, LANE), lambda b, i: (b, 0)), pl.BlockSpec((seq, LANE), lambda b, i: (b, 0)),
                  tile(LANE)],
        out_specs=tile(B_W),
        out_shape=jax.ShapeDtypeStruct((nbatch * seq, B_W), BF16),
        compiler_params=_params("parallel", "arbitrary"),
        name="nsa_prompt",
    )(qbp, qbrp, kcvc, kvsb, kvwb, g)


def _nsa_sample_kernel(pt_ref, qbp_ref, qbrp_ref, g_ref, cn_ref, sn_ref, wn_ref, win_ref, pe_ref, w_ref, *rest,
                       n_pages, n_new, past, nbp):
    del pt_ref
    cmp_pages = rest[:n_pages]
    slc_pages = rest[n_pages:2 * n_pages]
    o_ref = rest[2 * n_pages]
    msum_ref = rest[2 * n_pages + 1]
    tq = TOK_PAD
    rows = B_HEADS * tq
    page_rows = cmp_pages[0].shape[0]
    bpp = page_rows // NSA_BLOCK
    nb_past = n_pages * bpp

    for p, pg in enumerate(cmp_pages):
        msum_ref[p * bpp:(p + 1) * bpp, :] = jnp.sum(pg[...].reshape(bpp, NSA_BLOCK, LANE), axis=1)
    msum_ref[nb_past:nb_past + 1, :] = jnp.sum(cn_ref[...], axis=0, keepdims=True)
    msum_ref[nb_past + 1:nbp, :] = jnp.zeros((nbp - nb_past - 1, LANE), F32)
    mblk = msum_ref[...] * (1.0 / NSA_BLOCK) + jnp.mean(pe_ref[...], axis=0, keepdims=True)
    kcvc = jnp.dot(mblk, w_ref[...], preferred_element_type=F32, precision=lax.Precision.HIGHEST)

    qc = jnp.concatenate([qbp_ref[:, h * LANE:(h + 1) * LANE] for h in range(B_HEADS)], axis=0)
    o_c, chosen = _nsa_compressed_branch(qc, kcvc, past, tq)
    qr = jnp.concatenate([qbrp_ref[:, h * LANE:(h + 1) * LANE] for h in range(B_HEADS)], axis=0)

    def masked(s, ok):
        w = s.shape[1]
        return jnp.where(ok[None], s.reshape(B_HEADS, tq, w), NEG_INF).reshape(rows, w)

    tok_n = lax.broadcasted_iota(I32, (tq, NEW_ROWS), 0)
    j_n = lax.broadcasted_iota(I32, (tq, NEW_ROWS), 1)
    new_ok = (j_n <= tok_n) & (j_n < n_new)

    def attend(score_tiles, value_tiles):
        m = None
        for s in score_tiles:
            mx = jnp.max(s, axis=-1, keepdims=True)
            m = mx if m is None else jnp.maximum(m, mx)
        l = jnp.zeros((rows, 1), F32)
        acc = jnp.zeros((rows, LANE), F32)
        for s, v in zip(score_tiles, value_tiles):
            p = jnp.exp(s - m)
            l = l + jnp.sum(p, axis=-1, keepdims=True)
            acc = acc + _dot(p.astype(BF16), v)
        return acc / l

    tiles, vals = [], []
    for p, pg in enumerate(slc_pages):
        kv = pg[...].astype(BF16)
        ok = _expand_blocks(chosen, p * page_rows, page_rows) > 0.5
        tiles.append(masked(_dot_nt(qr, kv), ok))
        vals.append(kv)
    kv = sn_ref[...].astype(BF16)
    ok = (_expand_blocks(chosen, past, NEW_ROWS) > 0.5) & new_ok
    tiles.append(masked(_dot_nt(qr, kv), ok))
    vals.append(kv)
    o_s = attend(tiles, vals)

    wb = win_ref.shape[0]
    kv = win_ref[...].astype(BF16)
    r_w = lax.broadcasted_iota(I32, (tq, wb), 1)
    t_w = lax.broadcasted_iota(I32, (tq, wb), 0)
    tiles = [masked(_dot_nt(qr, kv), r_w > t_w + (wb - WINDOW))]
    vals = [kv]
    kv = wn_ref[...].astype(BF16)
    tiles.append(masked(_dot_nt(qr, kv), new_ok))
    vals.append(kv)
    o_w = attend(tiles, vals)

    o_ref[...] = _nsa_combine(g_ref[...], o_c, o_s, o_w, tq).astype(BF16)


def _nsa_sample(pt_flat, qbp8, qbrp8, g8, cn, sn, wn, state_win, pe_cat, w_bd, cache_cmp, cache_slc, layer, n_pages,
                n_new, past):
    nseq = qbp8.shape[0]
    page_rows = cache_cmp.shape[2]
    wb = state_win.shape[2]
    nb = -(-(past + n_new) // NSA_BLOCK)
    nbp = -(-nb // 16) * 16

    def page_spec(p):
        return pl.BlockSpec((None, None, page_rows, LANE), lambda b, pt, p=p: (layer, pt[b * n_pages + p], 0, 0))

    seq_spec = lambda r, w: pl.BlockSpec((None, r, w), lambda b, pt: (b, 0, 0))
    grid_spec = pltpu.PrefetchScalarGridSpec(
        num_scalar_prefetch=1,
        grid=(nseq,),
        in_specs=[seq_spec(TOK_PAD, 4 * LANE), seq_spec(TOK_PAD, 4 * LANE), seq_spec(TOK_PAD, LANE),
                  seq_spec(NEW_ROWS, LANE), seq_spec(NEW_ROWS, LANE), seq_spec(NEW_ROWS, LANE),
                  pl.BlockSpec((None, None, wb, LANE), lambda b, pt: (layer, b, 0, 0)),
                  pl.BlockSpec((NSA_BLOCK, LANE), lambda b, pt: (0, 0)),
                  pl.BlockSpec((LANE, LANE), lambda b, pt: (0, 0))]
        + [page_spec(p) for p in range(n_pages)] + [page_spec(p) for p in range(n_pages)],
        out_specs=seq_spec(TOK_PAD, B_W),
        scratch_shapes=[pltpu.VMEM((nbp, LANE), F32)],
    )
    return pl.pallas_call(
        functools.partial(_nsa_sample_kernel, n_pages=n_pages, n_new=n_new, past=past, nbp=nbp),
        grid_spec=grid_spec,
        out_shape=jax.ShapeDtypeStruct((nseq, TOK_PAD, B_W), BF16),
        compiler_params=_params("parallel"),
        name="nsa_sample",
    )(pt_flat, qbp8, qbrp8, g8, cn, sn, wn, state_win, pe_cat, w_bd, *([cache_cmp] * n_pages),
      *([cache_slc] * n_pages))


def _conv_prompt_kernel(c_ref, halo_ref, b_ref, w_ref, o_ref):
    i = pl.program_id(1)
    c = c_ref[...]
    rows = c.shape[0]
    halo = jnp.where(i > 0, halo_ref[...], 0.0)
    r = lax.broadcasted_iota(I32, c.shape, 0)
    c1 = jnp.where(r == 0, halo[7:8], pltpu.roll(c, 1, 0))
    c2 = jnp.where(r == 0, halo[6:7], jnp.where(r == 1, halo[7:8], pltpu.roll(c, 2, 0)))
    del rows
    z = w_ref[0:1] * c2 + w_ref[1:2] * c1 + w_ref[2:3] * c
    o_ref[...] = (b_ref[...] * z).astype(BF16)


def _conv_prompt(cin, bgate, conv_w, nbatch, seq, tm):
    nt = seq // tm
    hb = tm // 8
    tile = pl.BlockSpec((tm, C_W), lambda b, i: (b * nt + i, 0))
    return pl.pallas_call(
        _conv_prompt_kernel,
        grid=(nbatch, nt),
        in_specs=[tile, pl.BlockSpec((8, C_W), lambda b, i: (jnp.maximum((b * nt + i) * hb - 1, 0), 0)), tile,
                  pl.BlockSpec((CONV_W, C_W), lambda b, i: (0, 0))],
        out_specs=tile,
        out_shape=jax.ShapeDtypeStruct((nbatch * seq, C_W), BF16),
        compiler_params=_params("parallel", "parallel"),
        name="conv_prompt",
    )(cin, cin, bgate, conv_w)


def _conv_sample_kernel(c0_ref, c1_ref, c2_ref, b_ref, w_ref, o_ref):
    z = w_ref[0:1] * c0_ref[...] + w_ref[1:2] * c1_ref[...] + w_ref[2:3] * c2_ref[...]
    o_ref[...] = (b_ref[...] * z).astype(BF16)


def _conv_sample(c0, c1, c2, bgate, conv_w):
    return pl.pallas_call(
        _conv_sample_kernel,
        out_shape=jax.ShapeDtypeStruct(c0.shape, BF16),
        name="conv_sample",
    )(c0, c1, c2, bgate, conv_w)


def _mix_out_kernel(x_ref, mixed_ref, wout_ref, nw_ref, wq_ref, x_out_ref, q_out_ref):
    y = _dot(mixed_ref[...], wout_ref[...])
    x = x_ref[...] + _rms(y, nw_ref[NORM_MIX_POST:NORM_MIX_POST + 1])
    x_out_ref[...] = x
    h = _rms(x, nw_ref[NORM_X_PRE:NORM_X_PRE + 1]).astype(BF16)
    q_out_ref[...] = (_dot(h, wq_ref[...]) * Q_SCALE).astype(BF16)


def _mix_out(x, mixed, w_out, nw8, w_q, tm):
    n, d = x.shape
    row = lambda w: pl.BlockSpec((tm, w), lambda i: (i, 0))
    full = lambda a: pl.BlockSpec(a.shape, lambda i: (0, 0))
    return pl.pallas_call(
        _mix_out_kernel,
        grid=(n // tm,),
        in_specs=[row(d), row(d), full(w_out), full(nw8), full(w_q)],
        out_specs=[row(d), row(X_W)],
        out_shape=[jax.ShapeDtypeStruct((n, d), F32), jax.ShapeDtypeStruct((n, X_W), BF16)],
        compiler_params=_params("parallel"),
        name="mix_out",
    )(x, mixed, w_out, nw8, w_q)


def _cross_attn_kernel(q_ref, mk_ref, mv_ref, o_ref):
    q = q_ref[...]
    ts = q.shape[0]
    lane = lax.broadcasted_iota(I32, q.shape, 1)
    zero = jnp.zeros_like(q)
    qs = jnp.concatenate(
        [jnp.where((lane >= h * X_HD) & (lane < (h + 1) * X_HD), q, zero) for h in range(X_HEADS)], axis=0)
    s = _dot_nt(qs, mk_ref[...].astype(BF16))
    e = jnp.exp(s - jnp.max(s, axis=-1, keepdims=True))
    p = e / jnp.sum(e, axis=-1, keepdims=True)
    o = _dot(p.astype(BF16), mv_ref[...].astype(BF16))
    out = jnp.zeros((ts, X_W), F32)
    for h in range(X_HEADS):
        out = out + jnp.where((lane >= h * X_HD) & (lane < (h + 1) * X_HD), o[h * ts:(h + 1) * ts], 0.0)
    o_ref[...] = out.astype(BF16)


def _cross_attn(q, mk, mv, ts):
    nbatch, seq, _ = q.shape
    n_mem = mk.shape[1]
    return pl.pallas_call(
        _cross_attn_kernel,
        grid=(nbatch, seq // ts),
        in_specs=[pl.BlockSpec((None, ts, X_W), lambda b, i: (b, i, 0)),
                  pl.BlockSpec((None, n_mem, X_W), lambda b, i: (b, 0, 0)),
                  pl.BlockSpec((None, n_mem, X_W), lambda b, i: (b, 0, 0))],
        out_specs=pl.BlockSpec((None, ts, X_W), lambda b, i: (b, i, 0)),
        out_shape=jax.ShapeDtypeStruct((nbatch, seq, X_W), BF16),
        compiler_params=_params("parallel", "parallel"),
        name="cross_attn",
    )(q, mk, mv)


def _mem_kv_kernel(m_ref, nw_ref, w_ref, o_ref):
    h = _rms(m_ref[...], nw_ref[...]).astype(BF16)
    o_ref[...] = _dot(h, w_ref[...])


def _mem_kv(mem, nw, w_kv):
    n, d = mem.shape
    return pl.pallas_call(
        _mem_kv_kernel,
        out_shape=jax.ShapeDtypeStruct((n, 2 * X_W), F32),
        compiler_params=pltpu.CompilerParams(vmem_limit_bytes=VMEM_LIMIT),
        name="mem_kv",
    )(mem, nw, w_kv)


def _ffn_kernel(x_ref, o_ref, wo_ref, nw_ref, wg_ref, wu_ref, wd_ref, y_ref, x2_ref, h_ref, acc_ref):
    j = pl.program_id(1)

    @pl.when(j == 0)
    def _():
        x2 = x_ref[...] + _rms(_dot(o_ref[...], wo_ref[...]), nw_ref[NORM_X_POST:NORM_X_POST + 1])
        x2_ref[...] = x2
        h_ref[...] = _rms(x2, nw_ref[NORM_FFN_PRE:NORM_FFN_PRE + 1]).astype(BF16)
        acc_ref[...] = jnp.zeros_like(acc_ref)

    h = h_ref[...]
    g = _dot(h, wg_ref[...])
    u = _dot(h, wu_ref[...])
    a = (g * jax.nn.sigmoid(g)) * u
    acc_ref[...] += _dot(a.astype(BF16), wd_ref[...])

    @pl.when(j == pl.num_programs(1) - 1)
    def _():
        y_ref[...] = x2_ref[...] + _rms(acc_ref[...], nw_ref[NORM_FFN_POST:NORM_FFN_POST + 1])


def _ffn(x, o, w_o, nw8, w_g, w_u, w_d, tm, tf):
    n, d = x.shape
    dff = w_g.shape[1]
    row = lambda w: pl.BlockSpec((tm, w), lambda i, j: (i, 0))
    return pl.pallas_call(
        _ffn_kernel,
        grid=(n // tm, dff // tf),
        in_specs=[row(d), row(X_W), pl.BlockSpec(w_o.shape, lambda i, j: (0, 0)),
                  pl.BlockSpec(nw8.shape, lambda i, j: (0, 0)),
                  pl.BlockSpec((d, tf), lambda i, j: (0, j)), pl.BlockSpec((d, tf), lambda i, j: (0, j)),
                  pl.BlockSpec((tf, d), lambda i, j: (j, 0))],
        out_specs=row(d),
        out_shape=jax.ShapeDtypeStruct((n, d), F32),
        scratch_shapes=[pltpu.VMEM((tm, d), F32), pltpu.VMEM((tm, d), BF16), pltpu.VMEM((tm, d), F32)],
        compiler_params=_params("parallel", "arbitrary"),
        name="ffn",
    )(x, o, w_o, nw8, w_g, w_u, w_d)


def _rope_tables(pos):
    inv = ROPE_THETA ** (-jnp.arange(ROPE_HALF, dtype=F32) / ROPE_HALF)
    ang = pos.astype(F32)[:, None] * inv[None, :]
    cos, sin = jnp.cos(ang), jnp.sin(ang)
    n = pos.shape[0]
    rest = A_HD - 2 * ROPE_HALF
    z8 = jnp.zeros((n, ROPE_HALF), F32)
    c = jnp.concatenate([cos, cos, jnp.ones((n, rest), F32)], axis=1)
    s1 = jnp.concatenate([-sin, z8, jnp.zeros((n, rest), F32)], axis=1)
    s2 = jnp.concatenate([z8, sin, jnp.zeros((n, rest), F32)], axis=1)
    return jnp.concatenate([c, c, s1, s1, s2, s2], axis=1)


def _pad_rows(t, rows):
    return jnp.pad(t, ((0, 0), (0, rows - t.shape[1]), (0, 0)))


def kernel(x_prompt, x_sample, cache_diff_k, cache_diff_v, cache_nsa_cmp, cache_nsa_slc, state_nsa_win, state_conv,
           cache_mem_k, cache_mem_v, page_table, mem_prompt, norm_w, w_in, w_out, diff_lambda, diff_subln, nsa_pe,
           nsa_w_cmp, conv_w, w_q_mem, w_kv_mem, w_o_mem, w_ffn_up, w_ffn_down):
    depth = w_in.shape[0]
    nbp, seq, d = x_prompt.shape
    nbs, sseq, _ = x_sample.shape
    n_p, n_s = nbp * seq, nbs * sseq
    n_pool, page_rows = cache_diff_k.shape[1], cache_diff_k.shape[2]
    n_pages = page_table.shape[1]
    past = n_pages * page_rows
    n_mem = mem_prompt.shape[1]
    dff = w_ffn_down.shape[1]
    assert seq % 256 == 0 and seq >= WINDOW and sseq <= TOK_PAD and d == 1024

    tm = _pick_tile(n_p + n_s, 512)
    tf = _pick_tile(dff, 1408, 128)

    split = 2176
    w_in_p = jnp.concatenate(
        [w_in[:, :, :split], w_in[:, :, split + 3 * B_HEADS:], w_in[:, :, split:split + 3 * B_HEADS],
         jnp.zeros((depth, d, IN_COLS_PADDED - w_in.shape[2]), w_in.dtype)], axis=2).astype(BF16)
    w_out_b = w_out.astype(BF16)
    w_q_b = w_q_mem.astype(BF16)
    w_kv_b = w_kv_mem.astype(BF16)
    w_o_b = w_o_mem.astype(BF16)
    w_g_b = w_ffn_up[:, :, :dff].astype(BF16)
    w_u_b = w_ffn_up[:, :, dff:].astype(BF16)
    w_d_b = w_ffn_down.astype(BF16)
    nw8 = jnp.pad(norm_w, ((0, 0), (0, 1), (0, 0)))
    pe_cat = jnp.concatenate([nsa_pe[:, 0], nsa_pe[:, 1]], axis=-1)
    zb = jnp.zeros((depth, B_HD, B_HD), F32)
    w_bd = jnp.concatenate([jnp.concatenate([nsa_w_cmp[:, 0], zb], axis=2),
                            jnp.concatenate([zb, nsa_w_cmp[:, 1]], axis=2)], axis=1)
    subw = diff_subln.reshape(depth, 1, 2 * A_HD)

    pos = jnp.concatenate([jnp.tile(jnp.arange(seq, dtype=I32), nbp),
                           jnp.tile(past + jnp.arange(sseq, dtype=I32), nbs)])
    rope = _rope_tables(pos)
    pt_flat = page_table.reshape(-1).astype(I32)
    ck = cache_diff_k.reshape(depth, n_pool, page_rows, A_W)
    cv = cache_diff_v.reshape(depth, n_pool, page_rows, A_W)
    mem_flat = mem_prompt.reshape(nbp * n_mem, d)

    x = jnp.concatenate([x_prompt.reshape(n_p, d), x_sample.reshape(n_s, d)], axis=0)
    outs = [[] for _ in range(14)]

    def sample3(t, rows):
        return _pad_rows(t[n_p:].reshape(nbs, sseq, t.shape[1]), rows)

    for l in range(depth):
        lam_init = 0.8 - 0.6 * math.exp(-0.3 * l)
        (qa, ka, kab, va, vab, qbp, qbrp, kvc, kvs, kvsb, kvw, kvwb, g, cin, bgate) = _proj_in(
            x, norm_w[l, NORM_MIX_PRE:NORM_MIX_PRE + 1], w_in_p[l], rope, tm)

        oa_p = _diff_prompt(qa, kab, vab, diff_lambda[l], subw[l], lam_init, nbp, seq, 256)
        kcvc = _nsa_compress(kvc, pe_cat[l], w_bd[l], nbp, seq)
        ob_p = _nsa_prompt(qbp, qbrp, kcvc, kvsb, kvwb, g, nbp, seq, 128)
        oc_p = _conv_prompt(cin, bgate, conv_w[l], nbp, seq, _pick_tile(seq, 512))

        oa_s = _diff_sample(pt_flat, diff_lambda[l], subw[l], sample3(qa, TOK_PAD), sample3(ka, NEW_ROWS),
                            sample3(va, NEW_ROWS), ck, cv, l, lam_init, n_pages, sseq)
        ob_s = _nsa_sample(pt_flat, sample3(qbp, TOK_PAD), sample3(qbrp, TOK_PAD), sample3(g, TOK_PAD),
                           sample3(kvc, NEW_ROWS), sample3(kvs, NEW_ROWS), sample3(kvw, NEW_ROWS), state_nsa_win,
                           pe_cat[l], w_bd[l], cache_nsa_cmp, cache_nsa_slc, l, n_pages, sseq, past)
        c_all = jnp.concatenate([state_conv[l], cin[n_p:].reshape(nbs, sseq, C_W)], axis=1)
        oc_s = _conv_sample(*[c_all[:, j:j + sseq].reshape(n_s, C_W) for j in range(CONV_W)], bgate[n_p:],
                            conv_w[l])

        mixed = jnp.concatenate([
            jnp.concatenate([oa_p, ob_p, oc_p], axis=1),
            jnp.concatenate([oa_s[:, :sseq].reshape(n_s, A_W), ob_s[:, :sseq].reshape(n_s, B_W), oc_s], axis=1)],
            axis=0)
        x, qx = _mix_out(x, mixed, w_out_b[l], nw8[l], w_q_b[l], tm)

        mkv = _mem_kv(mem_flat, norm_w[l, NORM_MEM:NORM_MEM + 1], w_kv_b[l]).reshape(nbp, n_mem, 2 * X_W)
        mk_p, mv_p = mkv[:, :, :X_W], mkv[:, :, X_W:]
        ox_p = _cross_attn(qx[:n_p].reshape(nbp, seq, X_W), mk_p, mv_p, _pick_tile(seq, 512))
        ox_s = _cross_attn(_pad_rows(qx[n_p:].reshape(nbs, sseq, X_W), TOK_PAD),
                           cache_mem_k[l].reshape(nbs, n_mem, X_W), cache_mem_v[l].reshape(nbs, n_mem, X_W), TOK_PAD)
        ox = jnp.concatenate([ox_p.reshape(n_p, X_W), ox_s[:, :sseq].reshape(n_s, X_W)], axis=0)
        x = _ffn(x, ox, w_o_b[l], nw8[l], w_g_b[l], w_u_b[l], w_d_b[l], tm, tf)

        win_p = kvw[:n_p].reshape(nbp, seq, 2 * B_HD)[:, seq - WINDOW:]
        cin_p = cin[:n_p].reshape(nbp, seq, C_W)
        win_s = jnp.concatenate([state_nsa_win[l], kvw[n_p:].reshape(nbs, sseq, 2 * B_HD)], axis=1)[:, sseq:]
        layer_out = (
            ka[:n_p].reshape(nbp, seq, A_HEADS, 2 * A_HD), va[:n_p].reshape(nbp, seq, A_HEADS, 2 * A_HD),
            kvc[:n_p].reshape(nbp, seq, 2 * B_HD), kvs[:n_p].reshape(nbp, seq, 2 * B_HD), win_p,
            cin_p[:, seq - (CONV_W - 1):], mk_p.reshape(nbp, n_mem, X_HEADS, X_HD),
            mv_p.reshape(nbp, n_mem, X_HEADS, X_HD),
            ka[n_p:].reshape(nbs, sseq, A_HEADS, 2 * A_HD), va[n_p:].reshape(nbs, sseq, A_HEADS, 2 * A_HD),
            kvc[n_p:].reshape(nbs, sseq, 2 * B_HD), kvs[n_p:].reshape(nbs, sseq, 2 * B_HD), win_s, c_all[:, sseq:])
        for acc, t in zip(outs, layer_out):
            acc.append(t)

    stacked = [jnp.stack(t) for t in outs]
    return (x[:n_p].reshape(nbp, seq, d), x[n_p:].reshape(nbs, sseq, d), *stacked)
```

```python
import functools
import math

import jax
import jax.numpy as jnp
from jax import lax
from jax.experimental import pallas as pl
from jax.experimental.pallas import tpu as pltpu

F32 = jnp.float32
BF16 = jnp.bfloat16
I32 = jnp.int32

A_HEADS = 4
A_HD = 64
A_W = A_HEADS * 2 * A_HD
B_HEADS = 4
B_HD = 64
B_W = B_HEADS * B_HD
NSA_BLOCK = 64
NSA_TOPK = 16
WINDOW = 512
C_W = 256
CONV_W = 3
X_HEADS = 4
X_HD = 64
X_W = X_HEADS * X_HD
ROPE_THETA = 500000.0
ROPE_HALF = (A_HD // 4) // 2
EPS = 1e-6
NEG_INF = -1e30
FORCE_SCORE = 1e4
Q_SCALE = A_HD ** -0.5
LANE = 128
IN_COLS_PADDED = 3072
NEW_ROWS = 16
TOK_PAD = 8
VMEM_LIMIT = 56 * 1024 * 1024

NORM_MIX_PRE, NORM_MIX_POST, NORM_X_PRE, NORM_X_POST, NORM_FFN_PRE, NORM_FFN_POST, NORM_MEM = range(7)

NT_DIMS = (((1,), (1,)), ((), ()))


def _params(*sem):
    return pltpu.CompilerParams(dimension_semantics=sem, vmem_limit_bytes=VMEM_LIMIT)


def _pick_tile(n, target, mult=16):
    best = None
    for t in range(mult, min(n, target) + 1, mult):
        if n % t == 0:
            best = t
    assert best is not None, (n, target)
    return best


def _rms(x, w):
    return x * lax.rsqrt(jnp.mean(x * x, axis=-1, keepdims=True) + EPS) * w


def _dot(a, b):
    return jnp.dot(a, b, preferred_element_type=F32)


def _dot_nt(a, b):
    return lax.dot_general(a, b, NT_DIMS, preferred_element_type=F32)


def _rope(z, c, s1, s2):
    outs = []
    for j in range(z.shape[1] // LANE):
        t = z[:, j * LANE:(j + 1) * LANE]
        outs.append(t * c + pltpu.roll(t, LANE - ROPE_HALF, 1) * s1 + pltpu.roll(t, ROPE_HALF, 1) * s2)
    return outs[0] if len(outs) == 1 else jnp.concatenate(outs, axis=1)


def _softmax_update(s, v, m, l, acc):
    m_new = jnp.maximum(m, jnp.max(s, axis=-1, keepdims=True))
    p = jnp.exp(s - m_new)
    alpha = jnp.exp(m - m_new)
    l = alpha * l + jnp.sum(p, axis=-1, keepdims=True)
    acc = alpha * acc + _dot(p.astype(BF16), v)
    return m_new, l, acc


def _proj_in_kernel(x_ref, nw_ref, w_ref, rope_ref, qa_ref, ka_ref, kab_ref, va_ref, vab_ref, qbp_ref, qbrp_ref,
                    kvc_ref, kvs_ref, kvsb_ref, kvw_ref, kvwb_ref, g_ref, cin_ref, bgate_ref):
    h = _rms(x_ref[...], nw_ref[...]).astype(BF16)
    c = rope_ref[:, 0:LANE]
    s1 = rope_ref[:, LANE:2 * LANE]
    s2 = rope_ref[:, 2 * LANE:3 * LANE]

    def mm(a, b):
        return _dot(h, w_ref[:, a:b])

    qa_ref[...] = (_rope(mm(0, 512), c, s1, s2) * Q_SCALE).astype(BF16)
    zk = _rope(mm(512, 1024), c, s1, s2)
    ka_ref[...] = zk
    kab_ref[...] = zk.astype(BF16)
    zv = mm(1024, 1536)
    va_ref[...] = zv
    vab_ref[...] = zv.astype(BF16)

    zb = mm(1536, 1792) * Q_SCALE
    zbr = _rope(zb, c, s1, s2)
    first = lax.broadcasted_iota(I32, (zb.shape[0], LANE), 1) < B_HD

    def pad_heads(z):
        outs = []
        for j in range(2):
            t = z[:, j * LANE:(j + 1) * LANE]
            outs.append(jnp.where(first, t, 0.0))
            outs.append(jnp.where(first, pltpu.roll(t, B_HD, 1), 0.0))
        return jnp.concatenate(outs, axis=1)

    qbp_ref[...] = pad_heads(zb).astype(BF16)
    qbrp_ref[...] = pad_heads(zbr).astype(BF16)

    zkv = mm(1792, 2176)
    kvc_ref[...] = zkv[:, 0:LANE]
    ckv = jnp.where(first, c, 1.0)
    s1kv = jnp.where(first, s1, 0.0)
    s2kv = jnp.where(first, s2, 0.0)
    ks = _rope(zkv[:, LANE:2 * LANE], ckv, s1kv, s2kv)
    kvs_ref[...] = ks
    kvsb_ref[...] = ks.astype(BF16)
    kw = _rope(zkv[:, 2 * LANE:3 * LANE], ckv, s1kv, s2kv)
    kvw_ref[...] = kw
    kvwb_ref[...] = kw.astype(BF16)

    zc = mm(2176, 2944)
    cin_ref[...] = zc[:, 2 * C_W:3 * C_W] * zc[:, 0:C_W]
    bgate_ref[...] = zc[:, C_W:2 * C_W]
    g_ref[...] = mm(2944, 3072)


def _proj_in(x, nw, w, rope, tm):
    n, d = x.shape
    row = lambda wdt: pl.BlockSpec((tm, wdt), lambda i: (i, 0))
    widths = [(512, BF16), (512, F32), (512, BF16), (512, F32), (512, BF16), (512, BF16), (512, BF16),
              (128, F32), (128, F32), (128, BF16), (128, F32), (128, BF16), (128, F32), (C_W, F32), (C_W, F32)]
    return pl.pallas_call(
        _proj_in_kernel,
        grid=(n // tm,),
        in_specs=[row(d), pl.BlockSpec((1, d), lambda i: (0, 0)),
                  pl.BlockSpec((d, IN_COLS_PADDED), lambda i: (0, 0)), row(3 * LANE)],
        out_specs=[row(wd) for wd, _ in widths],
        out_shape=[jax.ShapeDtypeStruct((n, wd), dt) for wd, dt in widths],
        compiler_params=_params("parallel"),
        name="proj_in",
    )(x, nw, w, rope)


def _diff_lambda(lw):
    a = jnp.sum(lw[0:1] * lw[1:2], axis=-1, keepdims=True)
    b = jnp.sum(lw[2:3] * lw[3:4], axis=-1, keepdims=True)
    return jnp.exp(a) - jnp.exp(b)


def _diff_finish(o1, o2, lam, subw, one_minus_init):
    d = o1 - lam * o2
    return _rms(d, subw) * one_minus_init


def _diff_prompt_kernel(lw_ref, q_ref, k_ref, v_ref, subw_ref, o_ref, *, tq, tk, lam_init):
    qi = pl.program_id(2)
    q = q_ref[...]
    lane = lax.broadcasted_iota(I32, (tq, LANE), 1)
    zero = jnp.zeros_like(q)
    qs = jnp.concatenate([jnp.where(lane < A_HD, q, zero), jnp.where(lane >= A_HD, q, zero)], axis=0)
    rows = 2 * tq

    def step(j, carry, diag):
        m, l, acc = carry
        start = pl.multiple_of(j * tk, tk)
        s = _dot_nt(qs, k_ref[pl.ds(start, tk), :])
        if diag:
            r = lax.broadcasted_iota(I32, (rows, tk), 0) & (tq - 1)
            cidx = lax.broadcasted_iota(I32, (rows, tk), 1)
            s = jnp.where(start + cidx <= qi * tq + r, s, NEG_INF)
        return _softmax_update(s, v_ref[pl.ds(start, tk), :], m, l, acc)

    n_full = (qi * tq) // tk
    init = (jnp.full((rows, 1), NEG_INF, F32), jnp.zeros((rows, 1), F32), jnp.zeros((rows, LANE), F32))
    carry = lax.fori_loop(0, n_full, lambda j, cr: step(j, cr, False), init)
    m, l, acc = step(n_full, carry, True)
    o = acc / l
    lam = _diff_lambda(lw_ref[...]) + lam_init
    o_ref[...] = _diff_finish(o[:tq], o[tq:], lam, subw_ref[...], 1.0 - lam_init).astype(BF16)


def _diff_prompt(qa, kab, vab, lw, subw, lam_init, nbatch, seq, tq, tk):
    nq = seq // tq
    assert tk % tq == 0 and seq % tk == 0
    return pl.pallas_call(
        functools.partial(_diff_prompt_kernel, tq=tq, tk=tk, lam_init=lam_init),
        grid=(nbatch, A_HEADS, nq),
        in_specs=[pl.BlockSpec((4, A_HD), lambda b, h, i: (0, 0)),
                  pl.BlockSpec((tq, LANE), lambda b, h, i: (b * nq + i, h)),
                  pl.BlockSpec((seq, LANE), lambda b, h, i: (b, h)),
                  pl.BlockSpec((seq, LANE), lambda b, h, i: (b, h)),
                  pl.BlockSpec((1, LANE), lambda b, h, i: (0, 0))],
        out_specs=pl.BlockSpec((tq, LANE), lambda b, h, i: (b * nq + i, h)),
        out_shape=jax.ShapeDtypeStruct((nbatch * seq, A_W), BF16),
        compiler_params=_params("parallel", "parallel", "arbitrary"),
        name="diff_prompt",
    )(lw, qa, kab, vab, subw)


def _diff_sample_kernel(pt_ref, lw_ref, subw_ref, q_ref, kn_ref, vn_ref, *rest, n_pages, n_new, lam_init):
    del pt_ref
    k_pages = rest[:n_pages]
    v_pages = rest[n_pages:2 * n_pages]
    o_ref = rest[2 * n_pages]
    q = q_ref[...]
    lane = lax.broadcasted_iota(I32, (TOK_PAD, LANE), 1)
    pieces = []
    for h in range(A_HEADS):
        qh = q[:, h * LANE:(h + 1) * LANE]
        pieces += [jnp.where(lane < A_HD, qh, jnp.zeros_like(qh)), jnp.where(lane >= A_HD, qh, jnp.zeros_like(qh))]
    qs = jnp.concatenate(pieces, axis=0)
    rows = qs.shape[0]
    rows_per_head = 2 * TOK_PAD

    def own_head(width):
        r = lax.broadcasted_iota(I32, (rows, width), 0)
        c = lax.broadcasted_iota(I32, (rows, width), 1)
        return (c & (A_HEADS - 1)) == (r // rows_per_head), r, c

    ok_page, _, _ = own_head(k_pages[0].shape[0])
    s_past = [jnp.where(ok_page, _dot_nt(qs, kp[...].astype(BF16)), NEG_INF) for kp in k_pages]
    ok_new, r_new, c_new = own_head(n_new * A_HEADS)
    ok_new = ok_new & ((c_new // A_HEADS) <= (r_new & (TOK_PAD - 1)))
    s_new = jnp.where(ok_new, _dot_nt(qs, kn_ref[...].astype(BF16)), NEG_INF)
    m = jnp.max(s_new, axis=-1, keepdims=True)
    for s in s_past:
        m = jnp.maximum(m, jnp.max(s, axis=-1, keepdims=True))
    p_new = jnp.exp(s_new - m)
    l = jnp.sum(p_new, axis=-1, keepdims=True)
    acc = _dot(p_new.astype(BF16), vn_ref[...].astype(BF16))
    for s, vp in zip(s_past, v_pages):
        p = jnp.exp(s - m)
        l = l + jnp.sum(p, axis=-1, keepdims=True)
        acc = acc + _dot(p.astype(BF16), vp[...].astype(BF16))
    o = acc / l
    lam = _diff_lambda(lw_ref[...]) + lam_init
    outs = []
    for h in range(A_HEADS):
        o1 = o[(2 * h) * TOK_PAD:(2 * h + 1) * TOK_PAD]
        o2 = o[(2 * h + 1) * TOK_PAD:(2 * h + 2) * TOK_PAD]
        outs.append(_diff_finish(o1, o2, lam, subw_ref[...], 1.0 - lam_init))
    o_ref[...] = jnp.concatenate(outs, axis=1).astype(BF16)


def _diff_sample(pt_flat, lw, subw, q8, kn, vn, cache_k, cache_v, layer, lam_init, n_pages, n_new):
    nseq = q8.shape[0]
    page_rows = cache_k.shape[2]
    new_rows = n_new * A_HEADS
    assert new_rows % 8 == 0

    def page_spec(p):
        return pl.BlockSpec((None, None, page_rows, LANE), lambda b, pt, p=p: (layer, pt[b * n_pages + p], 0, 0))

    seq_spec = lambda r, w: pl.BlockSpec((None, r, w), lambda b, pt: (b, 0, 0))
    grid_spec = pltpu.PrefetchScalarGridSpec(
        num_scalar_prefetch=1,
        grid=(nseq,),
        in_specs=[pl.BlockSpec((4, A_HD), lambda b, pt: (0, 0)), pl.BlockSpec((1, LANE), lambda b, pt: (0, 0)),
                  seq_spec(TOK_PAD, A_W), seq_spec(new_rows, LANE), seq_spec(new_rows, LANE)]
        + [page_spec(p) for p in range(n_pages)] + [page_spec(p) for p in range(n_pages)],
        out_specs=seq_spec(TOK_PAD, A_W),
    )
    return pl.pallas_call(
        functools.partial(_diff_sample_kernel, n_pages=n_pages, n_new=n_new, lam_init=lam_init),
        grid_spec=grid_spec,
        out_shape=jax.ShapeDtypeStruct((nseq, TOK_PAD, A_W), BF16),
        compiler_params=_params("parallel"),
        name="diff_sample",
    )(pt_flat, lw, subw, q8, kn, vn, *([cache_k] * n_pages), *([cache_v] * n_pages))


def _nsa_compress_kernel(kv_ref, pe_ref, w_ref, o_ref, *, nb):
    kv = kv_ref[...]
    m = jnp.sum(kv.reshape(nb, NSA_BLOCK, LANE), axis=1) * (1.0 / NSA_BLOCK)
    m = m + jnp.mean(pe_ref[...], axis=0, keepdims=True)
    o_ref[...] = jnp.dot(m, w_ref[...], preferred_element_type=F32, precision=lax.Precision.HIGHEST)


def _nsa_compress(kvc, pe_cat, w_bd, nbatch, seq):
    nb = seq // NSA_BLOCK
    return pl.pallas_call(
        functools.partial(_nsa_compress_kernel, nb=nb),
        grid=(nbatch,),
        in_specs=[pl.BlockSpec((seq, LANE), lambda b: (b, 0)), pl.BlockSpec((NSA_BLOCK, LANE), lambda b: (0, 0)),
                  pl.BlockSpec((LANE, LANE), lambda b: (0, 0))],
        out_specs=pl.BlockSpec((None, nb, LANE), lambda b: (b, 0, 0)),
        out_shape=jax.ShapeDtypeStruct((nbatch, nb, LANE), F32),
        compiler_params=_params("parallel"),
        name="nsa_compress",
    )(kvc, pe_cat, w_bd)


def _select_blocks_transposed(qc, kc, pos0, tq):
    nb = kc.shape[0]
    rows = B_HEADS * tq
    sc = _dot_nt(kc, qc)
    qpos = pos0 + (lax.broadcasted_iota(I32, (nb, rows), 1) & (tq - 1))
    blk = lax.broadcasted_iota(I32, (nb, rows), 0)
    scm = jnp.where((blk + 1) * NSA_BLOCK <= qpos + 1, sc, NEG_INF)
    e = jnp.exp(scm - jnp.max(scm, axis=0, keepdims=True))
    pc = e / jnp.sum(e, axis=0, keepdims=True)
    pc = pc * jnp.where(qpos + 1 >= NSA_BLOCK, 1.0, 0.0)
    imp = pc[:, 0:tq]
    for h in range(1, B_HEADS):
        imp = imp + pc[:, h * tq:(h + 1) * tq]
    qpos_t = pos0 + lax.broadcasted_iota(I32, (nb, tq), 1)
    blk_t = lax.broadcasted_iota(I32, (nb, tq), 0)
    cur = (qpos_t // NSA_BLOCK) == blk_t
    valid_t = (blk_t + 1) * NSA_BLOCK <= qpos_t + 1
    sel = jnp.where(cur, FORCE_SCORE, jnp.where(valid_t, imp, NEG_INF))
    key = lax.bitcast_convert_type(sel, I32)
    key = jnp.where(key < 0, key ^ 0x7FFFFFFF, key)
    blk_col = lax.broadcasted_iota(I32, (nb, 1), 0)
    rank = jnp.zeros((nb, tq), I32)
    for i in range(nb):
        thr = key[i:i + 1, :] + jnp.where(blk_col > i, 1, 0)
        rank = rank + jnp.where(thr > key, 1, 0)
    chosen_t = jnp.where((rank < NSA_TOPK) & (sel > NEG_INF / 2), 1.0, 0.0).astype(BF16)
    eye = jnp.where(lax.broadcasted_iota(I32, (tq, tq), 0) == lax.broadcasted_iota(I32, (tq, tq), 1), 1.0, 0.0)
    return _dot_nt(eye.astype(BF16), chosen_t).astype(BF16)


def _nsa_compressed_branch(qc, kcvc, pos0, tq, transposed_select=False):
    nb = kcvc.shape[0]
    rows = B_HEADS * tq
    lane = lax.broadcasted_iota(I32, kcvc.shape, 1)
    kc = jnp.where(lane < B_HD, kcvc, 0.0).astype(BF16)
    vc = jnp.where(lane >= B_HD, kcvc, 0.0).astype(BF16)
    sc = _dot_nt(qc, kc)
    qpos = pos0 + (lax.broadcasted_iota(I32, (rows, nb), 0) & (tq - 1))
    blk = lax.broadcasted_iota(I32, (rows, nb), 1)
    valid = (blk + 1) * NSA_BLOCK <= qpos + 1
    scm = jnp.where(valid, sc, NEG_INF)
    e = jnp.exp(scm - jnp.max(scm, axis=-1, keepdims=True))
    pc = e / jnp.sum(e, axis=-1, keepdims=True)
    pc = pc * jnp.where(qpos + 1 >= NSA_BLOCK, 1.0, 0.0)
    o_c = _dot(pc.astype(BF16), vc)
    if transposed_select:
        return o_c, _select_blocks_transposed(qc, kc, pos0, tq)
    imp = jnp.sum(pc.reshape(B_HEADS, tq, nb), axis=0)
    qpos_t = pos0 + lax.broadcasted_iota(I32, (tq, nb), 0)
    blk_t = lax.broadcasted_iota(I32, (tq, nb), 1)
    cur = (qpos_t // NSA_BLOCK) == blk_t
    valid_t = (blk_t + 1) * NSA_BLOCK <= qpos_t + 1
    sel = jnp.where(cur, FORCE_SCORE, jnp.where(valid_t, imp, NEG_INF))
    key = lax.bitcast_convert_type(sel, I32)
    key = jnp.where(key < 0, key ^ 0x7FFFFFFF, key)
    blk_row = lax.broadcasted_iota(I32, (1, nb), 1)
    rank = jnp.zeros((tq, nb), I32)
    for i in range(nb):
        thr = key[:, i:i + 1] + jnp.where(blk_row > i, 1, 0)
        rank = rank + jnp.where(thr > key, 1, 0)
    chosen = (rank < NSA_TOPK) & (sel > NEG_INF / 2)
    return o_c, jnp.where(chosen, 1.0, 0.0).astype(BF16)


def _expand_blocks(chosen, first_tok, width):
    nb = chosen.shape[1]
    tok = first_tok + lax.broadcasted_iota(I32, (nb, width), 1)
    blk = lax.broadcasted_iota(I32, (nb, width), 0)
    e = jnp.where((tok // NSA_BLOCK) == blk, 1.0, 0.0).astype(BF16)
    return _dot(chosen, e)


def _nsa_combine(g, o_c, o_s, o_w, tq):
    gate = jax.nn.sigmoid(g)
    lane = lax.broadcasted_iota(I32, (tq, LANE), 1)
    comb = []
    for h in range(B_HEADS):
        sl = slice(h * tq, (h + 1) * tq)
        comb.append(gate[:, 3 * h:3 * h + 1] * o_c[sl] + gate[:, 3 * h + 1:3 * h + 2] * o_s[sl]
                    + gate[:, 3 * h + 2:3 * h + 3] * o_w[sl])
    groups = [jnp.where(lane < B_HD, pltpu.roll(comb[2 * j], B_HD, 1), comb[2 * j + 1]) for j in range(2)]
    return jnp.concatenate(groups, axis=1)


def _nsa_prompt_kernel(qbp_ref, qbrp_ref, kcvc_ref, kvs_ref, kvw_ref, g_ref, o_ref, *, tq, tk):
    qi = pl.program_id(1)
    pos0 = qi * tq
    rows = B_HEADS * tq
    qc = jnp.concatenate([qbp_ref[:, h * LANE:(h + 1) * LANE] for h in range(B_HEADS)], axis=0)
    o_c, chosen = _nsa_compressed_branch(qc, kcvc_ref[...], pos0, tq, transposed_select=(tq % LANE == 0))
    qr = jnp.concatenate([qbrp_ref[:, h * LANE:(h + 1) * LANE] for h in range(B_HEADS)], axis=0)

    def masked(s, ok):
        w = s.shape[1]
        return jnp.where(ok[None], s.reshape(B_HEADS, tq, w), NEG_INF).reshape(rows, w)

    def values(kv):
        lane = lax.broadcasted_iota(I32, kv.shape, 1)
        return jnp.where(lane < B_HD, jnp.ones_like(kv), kv)

    qpos_s = pos0 + lax.broadcasted_iota(I32, (tq, tk), 0)
    c_s = lax.broadcasted_iota(I32, (tq, tk), 1)

    def sel_step(j, carry):
        m, acc = carry
        start = pl.multiple_of(j * tk, tk)
        kv = kvs_ref[pl.ds(start, tk), :]
        ok = (_expand_blocks(chosen, start, tk) > 0.5) & (start + c_s <= qpos_s)
        s = masked(_dot_nt(qr, kv), ok)
        m_new = jnp.maximum(m, jnp.max(s, axis=-1, keepdims=True))
        p = jnp.exp(s - m_new)
        acc = jnp.exp(m - m_new) * acc + _dot(p.astype(BF16), values(kv))
        return m_new, acc

    n_chunks = (pos0 + tq - 1) // tk + 1
    init = (jnp.full((rows, 1), NEG_INF, F32), jnp.zeros((rows, LANE), F32))
    _, acc = lax.fori_loop(0, n_chunks, sel_step, init)
    o_s = acc / acc[:, 0:1]

    span = WINDOW + tq
    start = pl.multiple_of(jnp.maximum(pos0 - WINDOW, 0), tq)
    kv = kvw_ref[pl.ds(start, span), :]
    qpos_w = pos0 + lax.broadcasted_iota(I32, (tq, span), 0)
    tok_w = start + lax.broadcasted_iota(I32, (tq, span), 1)
    s = masked(_dot_nt(qr, kv), (tok_w <= qpos_w) & (tok_w > qpos_w - WINDOW))
    p = jnp.exp(s - jnp.max(s, axis=-1, keepdims=True))
    acc = _dot(p.astype(BF16), values(kv))
    o_w = acc / acc[:, 0:1]

    o_ref[...] = _nsa_combine(g_ref[...], o_c, o_s, o_w, tq).astype(BF16)


def _nsa_prompt(qbp, qbrp, kcvc, kvsb, kvwb, g, nbatch, seq, tq, tk):
    nq = seq // tq
    nb = seq // NSA_BLOCK
    assert tk % tq == 0 and seq % tk == 0 and seq >= WINDOW + tq and WINDOW % tq == 0
    tile = lambda w: pl.BlockSpec((tq, w), lambda b, i: (b * nq + i, 0))
    return pl.pallas_call(
        functools.partial(_nsa_prompt_kernel, tq=tq, tk=tk),
        grid=(nbatch, nq),
        in_specs=[tile(4 * LANE), tile(4 * LANE), pl.BlockSpec((None, nb, LANE), lambda b, i: (b, 0, 0)),
                  pl.BlockSpec((seq, LANE), lambda b, i: (b, 0)), pl.BlockSpec((seq, LANE), lambda b, i: (b, 0)),
                  tile(LANE)],
        out_specs=tile(B_W),
        out_shape=jax.ShapeDtypeStruct((nbatch * seq, B_W), BF16),
        compiler_params=_params("parallel", "arbitrary"),
        name="nsa_prompt",
    )(qbp, qbrp, kcvc, kvsb, kvwb, g)


def _nsa_sample_kernel(pt_ref, qbp_ref, qbrp_ref, g_ref, cn_ref, sn_ref, wn_ref, win_ref, pe_ref, w_ref, *rest,
                       n_pages, n_new, past, nbp):
    del pt_ref
    cmp_pages = rest[:n_pages]
    slc_pages = rest[n_pages:2 * n_pages]
    o_ref = rest[2 * n_pages]
    msum_ref = rest[2 * n_pages + 1]
    tq = TOK_PAD
    rows = B_HEADS * tq
    page_rows = cmp_pages[0].shape[0]
    bpp = page_rows // NSA_BLOCK
    nb_past = n_pages * bpp

    for p, pg in enumerate(cmp_pages):
        msum_ref[p * bpp:(p + 1) * bpp, :] = jnp.sum(pg[...].reshape(bpp, NSA_BLOCK, LANE), axis=1)
    msum_ref[nb_past:nb_past + 1, :] = jnp.sum(cn_ref[...], axis=0, keepdims=True)
    msum_ref[nb_past + 1:nbp, :] = jnp.zeros((nbp - nb_past - 1, LANE), F32)
    mblk = msum_ref[...] * (1.0 / NSA_BLOCK) + jnp.mean(pe_ref[...], axis=0, keepdims=True)
    kcvc = jnp.dot(mblk, w_ref[...], preferred_element_type=F32, precision=lax.Precision.HIGHEST)

    qc = jnp.concatenate([qbp_ref[:, h * LANE:(h + 1) * LANE] for h in range(B_HEADS)], axis=0)
    o_c, chosen = _nsa_compressed_branch(qc, kcvc, past, tq)
    qr = jnp.concatenate([qbrp_ref[:, h * LANE:(h + 1) * LANE] for h in range(B_HEADS)], axis=0)

    def masked(s, ok):
        w = s.shape[1]
        return jnp.where(ok[None], s.reshape(B_HEADS, tq, w), NEG_INF).reshape(rows, w)

    tok_n = lax.broadcasted_iota(I32, (tq, NEW_ROWS), 0)
    j_n = lax.broadcasted_iota(I32, (tq, NEW_ROWS), 1)
    new_ok = (j_n <= tok_n) & (j_n < n_new)

    def attend(score_tiles, value_tiles):
        m = None
        for s in score_tiles:
            mx = jnp.max(s, axis=-1, keepdims=True)
            m = mx if m is None else jnp.maximum(m, mx)
        l = jnp.zeros((rows, 1), F32)
        acc = jnp.zeros((rows, LANE), F32)
        for s, v in zip(score_tiles, value_tiles):
            p = jnp.exp(s - m)
            l = l + jnp.sum(p, axis=-1, keepdims=True)
            acc = acc + _dot(p.astype(BF16), v)
        return acc / l

    tiles, vals = [], []
    for p, pg in enumerate(slc_pages):
        kv = pg[...].astype(BF16)
        ok = _expand_blocks(chosen, p * page_rows, page_rows) > 0.5
        tiles.append(masked(_dot_nt(qr, kv), ok))
        vals.append(kv)
    kv = sn_ref[...].astype(BF16)
    ok = (_expand_blocks(chosen, past, NEW_ROWS) > 0.5) & new_ok
    tiles.append(masked(_dot_nt(qr, kv), ok))
    vals.append(kv)
    o_s = attend(tiles, vals)

    wb = win_ref.shape[0]
    kv = win_ref[...].astype(BF16)
    r_w = lax.broadcasted_iota(I32, (tq, wb), 1)
    t_w = lax.broadcasted_iota(I32, (tq, wb), 0)
    tiles = [masked(_dot_nt(qr, kv), r_w > t_w + (wb - WINDOW))]
    vals = [kv]
    kv = wn_ref[...].astype(BF16)
    tiles.append(masked(_dot_nt(qr, kv), new_ok))
    vals.append(kv)
    o_w = attend(tiles, vals)

    o_ref[...] = _nsa_combine(g_ref[...], o_c, o_s, o_w, tq).astype(BF16)


def _nsa_sample(pt_flat, qbp8, qbrp8, g8, cn, sn, wn, state_win, pe_cat, w_bd, cache_cmp, cache_slc, layer, n_pages,
                n_new, past):
    nseq = qbp8.shape[0]
    page_rows = cache_cmp.shape[2]
    wb = state_win.shape[2]
    nb = -(-(past + n_new) // NSA_BLOCK)
    nbp = -(-nb // 16) * 16

    def page_spec(p):
        return pl.BlockSpec((None, None, page_rows, LANE), lambda b, pt, p=p: (layer, pt[b * n_pages + p], 0, 0))

    seq_spec = lambda r, w: pl.BlockSpec((None, r, w), lambda b, pt: (b, 0, 0))
    grid_spec = pltpu.PrefetchScalarGridSpec(
        num_scalar_prefetch=1,
        grid=(nseq,),
        in_specs=[seq_spec(TOK_PAD, 4 * LANE), seq_spec(TOK_PAD, 4 * LANE), seq_spec(TOK_PAD, LANE),
                  seq_spec(NEW_ROWS, LANE), seq_spec(NEW_ROWS, LANE), seq_spec(NEW_ROWS, LANE),
                  pl.BlockSpec((None, None, wb, LANE), lambda b, pt: (layer, b, 0, 0)),
                  pl.BlockSpec((NSA_BLOCK, LANE), lambda b, pt: (0, 0)),
                  pl.BlockSpec((LANE, LANE), lambda b, pt: (0, 0))]
        + [page_spec(p) for p in range(n_pages)] + [page_spec(p) for p in range(n_pages)],
        out_specs=seq_spec(TOK_PAD, B_W),
        scratch_shapes=[pltpu.VMEM((nbp, LANE), F32)],
    )
    return pl.pallas_call(
        functools.partial(_nsa_sample_kernel, n_pages=n_pages, n_new=n_new, past=past, nbp=nbp),
        grid_spec=grid_spec,
        out_shape=jax.ShapeDtypeStruct((nseq, TOK_PAD, B_W), BF16),
        compiler_params=_params("parallel"),
        name="nsa_sample",
    )(pt_flat, qbp8, qbrp8, g8, cn, sn, wn, state_win, pe_cat, w_bd, *([cache_cmp] * n_pages),
      *([cache_slc] * n_pages))


def _conv_prompt_kernel(c_ref, halo_ref, b_ref, w_ref, o_ref):
    i = pl.program_id(1)
    c = c_ref[...]
    rows = c.shape[0]
    halo = jnp.where(i > 0, halo_ref[...], 0.0)
    r = lax.broadcasted_iota(I32, c.shape, 0)
    c1 = jnp.where(r == 0, halo[7:8], pltpu.roll(c, 1, 0))
    c2 = jnp.where(r == 0, halo[6:7], jnp.where(r == 1, halo[7:8], pltpu.roll(c, 2, 0)))
    del rows
    z = w_ref[0:1] * c2 + w_ref[1:2] * c1 + w_ref[2:3] * c
    o_ref[...] = (b_ref[...] * z).astype(BF16)


def _conv_prompt(cin, bgate, conv_w, nbatch, seq, tm):
    nt = seq // tm
    hb = tm // 8
    tile = pl.BlockSpec((tm, C_W), lambda b, i: (b * nt + i, 0))
    return pl.pallas_call(
        _conv_prompt_kernel,
        grid=(nbatch, nt),
        in_specs=[tile, pl.BlockSpec((8, C_W), lambda b, i: (jnp.maximum((b * nt + i) * hb - 1, 0), 0)), tile,
                  pl.BlockSpec((CONV_W, C_W), lambda b, i: (0, 0))],
        out_specs=tile,
        out_shape=jax.ShapeDtypeStruct((nbatch * seq, C_W), BF16),
        compiler_params=_params("parallel", "parallel"),
        name="conv_prompt",
    )(cin, cin, bgate, conv_w)


def _conv_sample_kernel(c0_ref, c1_ref, c2_ref, b_ref, w_ref, o_ref):
    z = w_ref[0:1] * c0_ref[...] + w_ref[1:2] * c1_ref[...] + w_ref[2:3] * c2_ref[...]
    o_ref[...] = (b_ref[...] * z).astype(BF16)


def _conv_sample(c0, c1, c2, bgate, conv_w):
    return pl.pallas_call(
        _conv_sample_kernel,
        out_shape=jax.ShapeDtypeStruct(c0.shape, BF16),
        name="conv_sample",
    )(c0, c1, c2, bgate, conv_w)


def _mix_out_kernel(x_ref, oap_ref, obp_ref, ocp_ref, oas_ref, obs_ref, ocs_ref, wout_ref, nw_ref, wq_ref, x_out_ref,
                    q_out_ref, *, n_ptiles):
    is_prompt = pl.program_id(0) < n_ptiles
    y = (_dot(jnp.where(is_prompt, oap_ref[...], oas_ref[...]), wout_ref[0:A_W])
         + _dot(jnp.where(is_prompt, obp_ref[...], obs_ref[...]), wout_ref[A_W:A_W + B_W])
         + _dot(jnp.where(is_prompt, ocp_ref[...], ocs_ref[...]), wout_ref[A_W + B_W:A_W + B_W + C_W]))
    x = x_ref[...] + _rms(y, nw_ref[NORM_MIX_POST:NORM_MIX_POST + 1])
    x_out_ref[...] = x
    h = _rms(x, nw_ref[NORM_X_PRE:NORM_X_PRE + 1]).astype(BF16)
    q_out_ref[...] = (_dot(h, wq_ref[...]) * Q_SCALE).astype(BF16)


def _mix_out(x, mixed_p, mixed_s, w_out, nw8, w_q, tm):
    n, d = x.shape
    n_ptiles = mixed_p[0].shape[0] // tm
    assert mixed_p[0].shape[0] % tm == 0 and mixed_s[0].shape[0] % tm == 0
    row = lambda w: pl.BlockSpec((tm, w), lambda i: (i, 0))
    prow = lambda a: pl.BlockSpec((tm, a.shape[1]), lambda i: (jnp.minimum(i, n_ptiles - 1), 0))
    srow = lambda a: pl.BlockSpec((tm, a.shape[1]), lambda i: (jnp.maximum(i - n_ptiles, 0), 0))
    full = lambda a: pl.BlockSpec(a.shape, lambda i: (0, 0))
    return pl.pallas_call(
        functools.partial(_mix_out_kernel, n_ptiles=n_ptiles),
        grid=(n // tm,),
        in_specs=[row(d)] + [prow(a) for a in mixed_p] + [srow(a) for a in mixed_s]
        + [full(w_out), full(nw8), full(w_q)],
        out_specs=[row(d), row(X_W)],
        out_shape=[jax.ShapeDtypeStruct((n, d), F32), jax.ShapeDtypeStruct((n, X_W), BF16)],
        compiler_params=_params("parallel"),
        name="mix_out",
    )(x, *mixed_p, *mixed_s, w_out, nw8, w_q)


def _cross_attn_kernel(q_ref, mk_ref, mv_ref, o_ref):
    q = q_ref[...]
    ts = q.shape[0]
    lane = lax.broadcasted_iota(I32, q.shape, 1)
    zero = jnp.zeros_like(q)
    qs = jnp.concatenate(
        [jnp.where((lane >= h * X_HD) & (lane < (h + 1) * X_HD), q, zero) for h in range(X_HEADS)], axis=0)
    s = _dot_nt(qs, mk_ref[...].astype(BF16))
    e = jnp.exp(s - jnp.max(s, axis=-1, keepdims=True))
    p = e / jnp.sum(e, axis=-1, keepdims=True)
    o = _dot(p.astype(BF16), mv_ref[...].astype(BF16))
    out = jnp.zeros((ts, X_W), F32)
    for h in range(X_HEADS):
        out = out + jnp.where((lane >= h * X_HD) & (lane < (h + 1) * X_HD), o[h * ts:(h + 1) * ts], 0.0)
    o_ref[...] = out.astype(BF16)


def _cross_attn(q, mk, mv, ts):
    nbatch, seq, _ = q.shape
    n_mem = mk.shape[1]
    return pl.pallas_call(
        _cross_attn_kernel,
        grid=(nbatch, seq // ts),
        in_specs=[pl.BlockSpec((None, ts, X_W), lambda b, i: (b, i, 0)),
                  pl.BlockSpec((None, n_mem, X_W), lambda b, i: (b, 0, 0)),
                  pl.BlockSpec((None, n_mem, X_W), lambda b, i: (b, 0, 0))],
        out_specs=pl.BlockSpec((None, ts, X_W), lambda b, i: (b, i, 0)),
        out_shape=jax.ShapeDtypeStruct((nbatch, seq, X_W), BF16),
        compiler_params=_params("parallel", "parallel"),
        name="cross_attn",
    )(q, mk, mv)


def _mem_kv_kernel(m_ref, nw_ref, w_ref, o_ref):
    h = _rms(m_ref[...], nw_ref[...]).astype(BF16)
    o_ref[...] = _dot(h, w_ref[...])


def _mem_kv(mem, nw, w_kv):
    n, d = mem.shape
    return pl.pallas_call(
        _mem_kv_kernel,
        out_shape=jax.ShapeDtypeStruct((n, 2 * X_W), F32),
        compiler_params=pltpu.CompilerParams(vmem_limit_bytes=VMEM_LIMIT),
        name="mem_kv",
    )(mem, nw, w_kv)


def _ffn_kernel(x_ref, o_ref, wo_ref, nw_ref, wg_ref, wu_ref, wd_ref, y_ref, x2_ref, h_ref, acc_ref):
    j = pl.program_id(1)

    @pl.when(j == 0)
    def _():
        x2 = x_ref[...] + _rms(_dot(o_ref[...], wo_ref[...]), nw_ref[NORM_X_POST:NORM_X_POST + 1])
        x2_ref[...] = x2
        h_ref[...] = _rms(x2, nw_ref[NORM_FFN_PRE:NORM_FFN_PRE + 1]).astype(BF16)
        acc_ref[...] = jnp.zeros_like(acc_ref)

    h = h_ref[...]
    g = _dot(h, wg_ref[...])
    u = _dot(h, wu_ref[...])
    a = (g * jax.nn.sigmoid(g)) * u
    acc_ref[...] += _dot(a.astype(BF16), wd_ref[...])

    @pl.when(j == pl.num_programs(1) - 1)
    def _():
        y_ref[...] = x2_ref[...] + _rms(acc_ref[...], nw_ref[NORM_FFN_POST:NORM_FFN_POST + 1])


def _ffn(x, o, w_o, nw8, w_g, w_u, w_d, tm, tf):
    n, d = x.shape
    dff = w_g.shape[1]
    row = lambda w: pl.BlockSpec((tm, w), lambda i, j: (i, 0))
    return pl.pallas_call(
        _ffn_kernel,
        grid=(n // tm, dff // tf),
        in_specs=[row(d), row(X_W), pl.BlockSpec(w_o.shape, lambda i, j: (0, 0)),
                  pl.BlockSpec(nw8.shape, lambda i, j: (0, 0)),
                  pl.BlockSpec((d, tf), lambda i, j: (0, j)), pl.BlockSpec((d, tf), lambda i, j: (0, j)),
                  pl.BlockSpec((tf, d), lambda i, j: (j, 0))],
        out_specs=row(d),
        out_shape=jax.ShapeDtypeStruct((n, d), F32),
        scratch_shapes=[pltpu.VMEM((tm, d), F32), pltpu.VMEM((tm, d), BF16), pltpu.VMEM((tm, d), F32)],
        compiler_params=_params("parallel", "arbitrary"),
        name="ffn",
    )(x, o, w_o, nw8, w_g, w_u, w_d)


def _rope_tables(pos):
    inv = ROPE_THETA ** (-jnp.arange(ROPE_HALF, dtype=F32) / ROPE_HALF)
    ang = pos.astype(F32)[:, None] * inv[None, :]
    cos, sin = jnp.cos(ang), jnp.sin(ang)
    n = pos.shape[0]
    rest = A_HD - 2 * ROPE_HALF
    z8 = jnp.zeros((n, ROPE_HALF), F32)
    c = jnp.concatenate([cos, cos, jnp.ones((n, rest), F32)], axis=1)
    s1 = jnp.concatenate([-sin, z8, jnp.zeros((n, rest), F32)], axis=1)
    s2 = jnp.concatenate([z8, sin, jnp.zeros((n, rest), F32)], axis=1)
    return jnp.concatenate([c, c, s1, s1, s2, s2], axis=1)


def _pad_rows(t, rows):
    return jnp.pad(t, ((0, 0), (0, rows - t.shape[1]), (0, 0)))


def kernel(x_prompt, x_sample, cache_diff_k, cache_diff_v, cache_nsa_cmp, cache_nsa_slc, state_nsa_win, state_conv,
           cache_mem_k, cache_mem_v, page_table, mem_prompt, norm_w, w_in, w_out, diff_lambda, diff_subln, nsa_pe,
           nsa_w_cmp, conv_w, w_q_mem, w_kv_mem, w_o_mem, w_ffn_up, w_ffn_down):
    depth = w_in.shape[0]
    nbp, seq, d = x_prompt.shape
    nbs, sseq, _ = x_sample.shape
    n_p, n_s = nbp * seq, nbs * sseq
    n_pool, page_rows = cache_diff_k.shape[1], cache_diff_k.shape[2]
    n_pages = page_table.shape[1]
    past = n_pages * page_rows
    n_mem = mem_prompt.shape[1]
    dff = w_ffn_down.shape[1]
    assert seq % 256 == 0 and seq >= WINDOW and sseq <= TOK_PAD and d == 1024

    tm = _pick_tile(math.gcd(n_p, n_s), 512)
    tf = _pick_tile(dff, 1408, 128)

    split = 2176
    w_in_p = jnp.concatenate(
        [w_in[:, :, :split], w_in[:, :, split + 3 * B_HEADS:], w_in[:, :, split:split + 3 * B_HEADS],
         jnp.zeros((depth, d, IN_COLS_PADDED - w_in.shape[2]), w_in.dtype)], axis=2).astype(BF16)
    w_out_b = w_out.astype(BF16)
    w_q_b = w_q_mem.astype(BF16)
    w_kv_b = w_kv_mem.astype(BF16)
    w_o_b = w_o_mem.astype(BF16)
    w_g_b = w_ffn_up[:, :, :dff].astype(BF16)
    w_u_b = w_ffn_up[:, :, dff:].astype(BF16)
    w_d_b = w_ffn_down.astype(BF16)
    nw8 = jnp.pad(norm_w, ((0, 0), (0, 1), (0, 0)))
    pe_cat = jnp.concatenate([nsa_pe[:, 0], nsa_pe[:, 1]], axis=-1)
    zb = jnp.zeros((depth, B_HD, B_HD), F32)
    w_bd = jnp.concatenate([jnp.concatenate([nsa_w_cmp[:, 0], zb], axis=2),
                            jnp.concatenate([zb, nsa_w_cmp[:, 1]], axis=2)], axis=1)
    subw = diff_subln.reshape(depth, 1, 2 * A_HD)

    pos = jnp.concatenate([jnp.tile(jnp.arange(seq, dtype=I32), nbp),
                           jnp.tile(past + jnp.arange(sseq, dtype=I32), nbs)])
    rope = _rope_tables(pos)
    pt_flat = page_table.reshape(-1).astype(I32)
    ck = cache_diff_k.reshape(depth, n_pool, page_rows * A_HEADS, 2 * A_HD)
    cv = cache_diff_v.reshape(depth, n_pool, page_rows * A_HEADS, 2 * A_HD)
    mem_flat = mem_prompt.reshape(nbp * n_mem, d)

    x = jnp.concatenate([x_prompt.reshape(n_p, d), x_sample.reshape(n_s, d)], axis=0)
    outs = [[] for _ in range(14)]

    def sample3(t, rows):
        return _pad_rows(t[n_p:].reshape(nbs, sseq, t.shape[1]), rows)

    for l in range(depth):
        lam_init = 0.8 - 0.6 * math.exp(-0.3 * l)
        (qa, ka, kab, va, vab, qbp, qbrp, kvc, kvs, kvsb, kvw, kvwb, g, cin, bgate) = _proj_in(
            x, norm_w[l, NORM_MIX_PRE:NORM_MIX_PRE + 1], w_in_p[l], rope, tm)

        oa_p = _diff_prompt(qa, kab, vab, diff_lambda[l], subw[l], lam_init, nbp, seq, 256, 512)
        kcvc = _nsa_compress(kvc, pe_cat[l], w_bd[l], nbp, seq)
        ob_p = _nsa_prompt(qbp, qbrp, kcvc, kvsb, kvwb, g, nbp, seq, 128, 512)
        oc_p = _conv_prompt(cin, bgate, conv_w[l], nbp, seq, _pick_tile(seq, 512))

        new_heads = lambda t: t[n_p:].reshape(nbs, sseq * A_HEADS, 2 * A_HD)
        oa_s = _diff_sample(pt_flat, diff_lambda[l], subw[l], sample3(qa, TOK_PAD), new_heads(ka), new_heads(va),
                            ck, cv, l, lam_init, n_pages, sseq)
        ob_s = _nsa_sample(pt_flat, sample3(qbp, TOK_PAD), sample3(qbrp, TOK_PAD), sample3(g, TOK_PAD),
                           sample3(kvc, NEW_ROWS), sample3(kvs, NEW_ROWS), sample3(kvw, NEW_ROWS), state_nsa_win,
                           pe_cat[l], w_bd[l], cache_nsa_cmp, cache_nsa_slc, l, n_pages, sseq, past)
        c_all = jnp.concatenate([state_conv[l], cin[n_p:].reshape(nbs, sseq, C_W)], axis=1)
        oc_s = _conv_sample(*[c_all[:, j:j + sseq].reshape(n_s, C_W) for j in range(CONV_W)], bgate[n_p:],
                            conv_w[l])

        x, qx = _mix_out(x, (oa_p, ob_p, oc_p),
                         (oa_s[:, :sseq].reshape(n_s, A_W), ob_s[:, :sseq].reshape(n_s, B_W), oc_s),
                         w_out_b[l], nw8[l], w_q_b[l], tm)

        mkv = _mem_kv(mem_flat, norm_w[l, NORM_MEM:NORM_MEM + 1], w_kv_b[l]).reshape(nbp, n_mem, 2 * X_W)
        mk_p, mv_p = mkv[:, :, :X_W], mkv[:, :, X_W:]
        ox_p = _cross_attn(qx[:n_p].reshape(nbp, seq, X_W), mk_p, mv_p, _pick_tile(seq, 512))
        ox_s = _cross_attn(_pad_rows(qx[n_p:].reshape(nbs, sseq, X_W), TOK_PAD),
                           cache_mem_k[l].reshape(nbs, n_mem, X_W), cache_mem_v[l].reshape(nbs, n_mem, X_W), TOK_PAD)
        ox = jnp.concatenate([ox_p.reshape(n_p, X_W), ox_s[:, :sseq].reshape(n_s, X_W)], axis=0)
        x = _ffn(x, ox, w_o_b[l], nw8[l], w_g_b[l], w_u_b[l], w_d_b[l], tm, tf)

        win_p = kvw[:n_p].reshape(nbp, seq, 2 * B_HD)[:, seq - WINDOW:]
        cin_p = cin[:n_p].reshape(nbp, seq, C_W)
        win_s = jnp.concatenate([state_nsa_win[l], kvw[n_p:].reshape(nbs, sseq, 2 * B_HD)], axis=1)[:, sseq:]
        layer_out = (
            ka[:n_p].reshape(nbp, seq, A_HEADS, 2 * A_HD), va[:n_p].reshape(nbp, seq, A_HEADS, 2 * A_HD),
            kvc[:n_p].reshape(nbp, seq, 2 * B_HD), kvs[:n_p].reshape(nbp, seq, 2 * B_HD), win_p,
            cin_p[:, seq - (CONV_W - 1):], mk_p.reshape(nbp, n_mem, X_HEADS, X_HD),
            mv_p.reshape(nbp, n_mem, X_HEADS, X_HD),
            ka[n_p:].reshape(nbs, sseq, A_HEADS, 2 * A_HD), va[n_p:].reshape(nbs, sseq, A_HEADS, 2 * A_HD),
            kvc[n_p:].reshape(nbs, sseq, 2 * B_HD), kvs[n_p:].reshape(nbs, sseq, 2 * B_HD), win_s, c_all[:, sseq:])
        for acc, t in zip(outs, layer_out):
            acc.append(t)

    stacked = [jnp.stack(t) for t in outs]
    return (x[:n_p].reshape(nbp, seq, d), x[n_p:].reshape(nbs, sseq, d), *stacked)
```

```python
import functools
import math

import jax
import jax.numpy as jnp
from jax import lax
from jax.experimental import pallas as pl
from jax.experimental.pallas import tpu as pltpu

F32 = jnp.float32
BF16 = jnp.bfloat16
I32 = jnp.int32

A_HEADS = 4
A_HD = 64
A_W = A_HEADS * 2 * A_HD
B_HEADS = 4
B_HD = 64
B_W = B_HEADS * B_HD
NSA_BLOCK = 64
NSA_TOPK = 16
WINDOW = 512
C_W = 256
CONV_W = 3
X_HEADS = 4
X_HD = 64
X_W = X_HEADS * X_HD
ROPE_THETA = 500000.0
ROPE_HALF = (A_HD // 4) // 2
EPS = 1e-6
NEG_INF = -1e30
FORCE_SCORE = 1e4
Q_SCALE = A_HD ** -0.5
LANE = 128
IN_COLS_PADDED = 3072
NEW_ROWS = 16
TOK_PAD = 8
VMEM_LIMIT = 56 * 1024 * 1024

NORM_MIX_PRE, NORM_MIX_POST, NORM_X_PRE, NORM_X_POST, NORM_FFN_PRE, NORM_FFN_POST, NORM_MEM = range(7)

NT_DIMS = (((1,), (1,)), ((), ()))


def _params(*sem):
    return pltpu.CompilerParams(dimension_semantics=sem, vmem_limit_bytes=VMEM_LIMIT)


def _pick_tile(n, target, mult=16):
    best = None
    for t in range(mult, min(n, target) + 1, mult):
        if n % t == 0:
            best = t
    assert best is not None, (n, target)
    return best


def _rms(x, w):
    return x * lax.rsqrt(jnp.mean(x * x, axis=-1, keepdims=True) + EPS) * w


def _dot(a, b):
    return jnp.dot(a, b, preferred_element_type=F32)


def _dot_nt(a, b):
    return lax.dot_general(a, b, NT_DIMS, preferred_element_type=F32)


def _rope(z, c, s1, s2):
    outs = []
    for j in range(z.shape[1] // LANE):
        t = z[:, j * LANE:(j + 1) * LANE]
        outs.append(t * c + pltpu.roll(t, LANE - ROPE_HALF, 1) * s1 + pltpu.roll(t, ROPE_HALF, 1) * s2)
    return outs[0] if len(outs) == 1 else jnp.concatenate(outs, axis=1)


def _softmax_update(s, v, m, l, acc):
    m_new = jnp.maximum(m, jnp.max(s, axis=-1, keepdims=True))
    p = jnp.exp(s - m_new)
    alpha = jnp.exp(m - m_new)
    l = alpha * l + jnp.sum(p, axis=-1, keepdims=True)
    acc = alpha * acc + _dot(p.astype(BF16), v)
    return m_new, l, acc


def _proj_in_kernel(x_ref, nw_ref, w_ref, rope_ref, qa_ref, ka_ref, kab_ref, va_ref, vab_ref, qbp_ref, qbrp_ref,
                    kvc_ref, kvs_ref, kvsb_ref, kvw_ref, kvwb_ref, g_ref, cin_ref, bgate_ref):
    h = _rms(x_ref[...], nw_ref[...]).astype(BF16)
    c = rope_ref[:, 0:LANE]
    s1 = rope_ref[:, LANE:2 * LANE]
    s2 = rope_ref[:, 2 * LANE:3 * LANE]

    def mm(a, b):
        return _dot(h, w_ref[:, a:b])

    qa_ref[...] = (_rope(mm(0, 512), c, s1, s2) * Q_SCALE).astype(BF16)
    zk = _rope(mm(512, 1024), c, s1, s2)
    ka_ref[...] = zk
    kab_ref[...] = zk.astype(BF16)
    zv = mm(1024, 1536)
    va_ref[...] = zv
    vab_ref[...] = zv.astype(BF16)

    zb = mm(1536, 1792) * Q_SCALE
    zbr = _rope(zb, c, s1, s2)
    first = lax.broadcasted_iota(I32, (zb.shape[0], LANE), 1) < B_HD

    def pad_heads(z):
        outs = []
        for j in range(2):
            t = z[:, j * LANE:(j + 1) * LANE]
            outs.append(jnp.where(first, t, 0.0))
            outs.append(jnp.where(first, pltpu.roll(t, B_HD, 1), 0.0))
        return jnp.concatenate(outs, axis=1)

    qbp_ref[...] = pad_heads(zb).astype(BF16)
    qbrp_ref[...] = pad_heads(zbr).astype(BF16)

    zkv = mm(1792, 2176)
    kvc_ref[...] = zkv[:, 0:LANE]
    ckv = jnp.where(first, c, 1.0)
    s1kv = jnp.where(first, s1, 0.0)
    s2kv = jnp.where(first, s2, 0.0)
    ks = _rope(zkv[:, LANE:2 * LANE], ckv, s1kv, s2kv)
    kvs_ref[...] = ks
    kvsb_ref[...] = ks.astype(BF16)
    kw = _rope(zkv[:, 2 * LANE:3 * LANE], ckv, s1kv, s2kv)
    kvw_ref[...] = kw
    kvwb_ref[...] = kw.astype(BF16)

    zc = mm(2176, 2944)
    cin_ref[...] = zc[:, 2 * C_W:3 * C_W] * zc[:, 0:C_W]
    bgate_ref[...] = zc[:, C_W:2 * C_W]
    g_ref[...] = mm(2944, 3072)


def _proj_in(x, nw, w, rope, tm):
    n, d = x.shape
    row = lambda wdt: pl.BlockSpec((tm, wdt), lambda i: (i, 0))
    widths = [(512, BF16), (512, F32), (512, BF16), (512, F32), (512, BF16), (512, BF16), (512, BF16),
              (128, F32), (128, F32), (128, BF16), (128, F32), (128, BF16), (128, F32), (C_W, F32), (C_W, F32)]
    return pl.pallas_call(
        _proj_in_kernel,
        grid=(n // tm,),
        in_specs=[row(d), pl.BlockSpec((1, d), lambda i: (0, 0)),
                  pl.BlockSpec((d, IN_COLS_PADDED), lambda i: (0, 0)), row(3 * LANE)],
        out_specs=[row(wd) for wd, _ in widths],
        out_shape=[jax.ShapeDtypeStruct((n, wd), dt) for wd, dt in widths],
        compiler_params=_params("parallel"),
        name="proj_in",
    )(x, nw, w, rope)


def _diff_lambda(lw):
    a = jnp.sum(lw[0:1] * lw[1:2], axis=-1, keepdims=True)
    b = jnp.sum(lw[2:3] * lw[3:4], axis=-1, keepdims=True)
    return jnp.exp(a) - jnp.exp(b)


def _diff_finish(o1, o2, lam, subw, one_minus_init):
    d = o1 - lam * o2
    return _rms(d, subw) * one_minus_init


def _diff_prompt_kernel(lw_ref, q_ref, k_ref, v_ref, subw_ref, o_ref, *, tq, tk, lam_init):
    qi = pl.program_id(2)
    q = q_ref[...]
    lane = lax.broadcasted_iota(I32, (tq, LANE), 1)
    zero = jnp.zeros_like(q)
    qs = jnp.concatenate([jnp.where(lane < A_HD, q, zero), jnp.where(lane >= A_HD, q, zero)], axis=0)
    rows = 2 * tq

    def step(j, carry, diag):
        m, l, acc = carry
        start = pl.multiple_of(j * tk, tk)
        s = _dot_nt(qs, k_ref[pl.ds(start, tk), :])
        if diag:
            r = lax.broadcasted_iota(I32, (rows, tk), 0) & (tq - 1)
            cidx = lax.broadcasted_iota(I32, (rows, tk), 1)
            s = jnp.where(start + cidx <= qi * tq + r, s, NEG_INF)
        return _softmax_update(s, v_ref[pl.ds(start, tk), :], m, l, acc)

    n_full = (qi * tq) // tk
    init = (jnp.full((rows, 1), NEG_INF, F32), jnp.zeros((rows, 1), F32), jnp.zeros((rows, LANE), F32))
    carry = lax.fori_loop(0, n_full, lambda j, cr: step(j, cr, False), init)
    m, l, acc = step(n_full, carry, True)
    o = acc / l
    lam = _diff_lambda(lw_ref[...]) + lam_init
    o_ref[...] = _diff_finish(o[:tq], o[tq:], lam, subw_ref[...], 1.0 - lam_init).astype(BF16)


def _diff_prompt(qa, kab, vab, lw, subw, lam_init, nbatch, seq, tq, tk):
    nq = seq // tq
    assert tk % tq == 0 and seq % tk == 0
    return pl.pallas_call(
        functools.partial(_diff_prompt_kernel, tq=tq, tk=tk, lam_init=lam_init),
        grid=(nbatch, A_HEADS, nq),
        in_specs=[pl.BlockSpec((4, A_HD), lambda b, h, i: (0, 0)),
                  pl.BlockSpec((tq, LANE), lambda b, h, i: (b * nq + i, h)),
                  pl.BlockSpec((seq, LANE), lambda b, h, i: (b, h)),
                  pl.BlockSpec((seq, LANE), lambda b, h, i: (b, h)),
                  pl.BlockSpec((1, LANE), lambda b, h, i: (0, 0))],
        out_specs=pl.BlockSpec((tq, LANE), lambda b, h, i: (b * nq + i, h)),
        out_shape=jax.ShapeDtypeStruct((nbatch * seq, A_W), BF16),
        compiler_params=_params("parallel", "parallel", "arbitrary"),
        name="diff_prompt",
    )(lw, qa, kab, vab, subw)


def _diff_sample_body(lw_ref, subw_ref, q_ref, kn_ref, vn_ref, k_pages, v_pages, o_ref, *, n_new, lam_init):
    q = q_ref[...]
    lane = lax.broadcasted_iota(I32, (TOK_PAD, LANE), 1)
    pieces = []
    for h in range(A_HEADS):
        qh = q[:, h * LANE:(h + 1) * LANE]
        pieces += [jnp.where(lane < A_HD, qh, jnp.zeros_like(qh)), jnp.where(lane >= A_HD, qh, jnp.zeros_like(qh))]
    qs = jnp.concatenate(pieces, axis=0)
    rows = qs.shape[0]
    rows_per_head = 2 * TOK_PAD

    def own_head(width):
        r = lax.broadcasted_iota(I32, (rows, width), 0)
        c = lax.broadcasted_iota(I32, (rows, width), 1)
        return (c & (A_HEADS - 1)) == (r // rows_per_head), r, c

    ok_page, _, _ = own_head(k_pages[0].shape[0])
    s_past = [jnp.where(ok_page, _dot_nt(qs, kp[...].astype(BF16)), NEG_INF) for kp in k_pages]
    ok_new, r_new, c_new = own_head(n_new * A_HEADS)
    ok_new = ok_new & ((c_new // A_HEADS) <= (r_new & (TOK_PAD - 1)))
    s_new = jnp.where(ok_new, _dot_nt(qs, kn_ref[...].astype(BF16)), NEG_INF)
    m = jnp.max(s_new, axis=-1, keepdims=True)
    for s in s_past:
        m = jnp.maximum(m, jnp.max(s, axis=-1, keepdims=True))
    p_new = jnp.exp(s_new - m)
    l = jnp.sum(p_new, axis=-1, keepdims=True)
    acc = _dot(p_new.astype(BF16), vn_ref[...].astype(BF16))
    for s, vp in zip(s_past, v_pages):
        p = jnp.exp(s - m)
        l = l + jnp.sum(p, axis=-1, keepdims=True)
        acc = acc + _dot(p.astype(BF16), vp[...].astype(BF16))
    o = acc / l
    lam = _diff_lambda(lw_ref[...]) + lam_init
    outs = []
    for h in range(A_HEADS):
        o1 = o[(2 * h) * TOK_PAD:(2 * h + 1) * TOK_PAD]
        o2 = o[(2 * h + 1) * TOK_PAD:(2 * h + 2) * TOK_PAD]
        outs.append(_diff_finish(o1, o2, lam, subw_ref[...], 1.0 - lam_init))
    o_ref[...] = jnp.concatenate(outs, axis=1).astype(BF16)


def _nsa_compress_kernel(kv_ref, pe_ref, w_ref, o_ref, *, nb):
    kv = kv_ref[...]
    m = jnp.sum(kv.reshape(nb, NSA_BLOCK, LANE), axis=1) * (1.0 / NSA_BLOCK)
    m = m + jnp.mean(pe_ref[...], axis=0, keepdims=True)
    o_ref[...] = jnp.dot(m, w_ref[...], preferred_element_type=F32, precision=lax.Precision.HIGHEST)


def _nsa_compress(kvc, pe_cat, w_bd, nbatch, seq):
    nb = seq // NSA_BLOCK
    return pl.pallas_call(
        functools.partial(_nsa_compress_kernel, nb=nb),
        grid=(nbatch,),
        in_specs=[pl.BlockSpec((seq, LANE), lambda b: (b, 0)), pl.BlockSpec((NSA_BLOCK, LANE), lambda b: (0, 0)),
                  pl.BlockSpec((LANE, LANE), lambda b: (0, 0))],
        out_specs=pl.BlockSpec((None, nb, LANE), lambda b: (b, 0, 0)),
        out_shape=jax.ShapeDtypeStruct((nbatch, nb, LANE), F32),
        compiler_params=_params("parallel"),
        name="nsa_compress",
    )(kvc, pe_cat, w_bd)


def _select_blocks_transposed(qc, kc, pos0, tq):
    nb = kc.shape[0]
    rows = B_HEADS * tq
    sc = _dot_nt(kc, qc)
    qpos = pos0 + (lax.broadcasted_iota(I32, (nb, rows), 1) & (tq - 1))
    blk = lax.broadcasted_iota(I32, (nb, rows), 0)
    scm = jnp.where((blk + 1) * NSA_BLOCK <= qpos + 1, sc, NEG_INF)
    e = jnp.exp(scm - jnp.max(scm, axis=0, keepdims=True))
    pc = e / jnp.sum(e, axis=0, keepdims=True)
    pc = pc * jnp.where(qpos + 1 >= NSA_BLOCK, 1.0, 0.0)
    imp = pc[:, 0:tq]
    for h in range(1, B_HEADS):
        imp = imp + pc[:, h * tq:(h + 1) * tq]
    qpos_t = pos0 + lax.broadcasted_iota(I32, (nb, tq), 1)
    blk_t = lax.broadcasted_iota(I32, (nb, tq), 0)
    cur = (qpos_t // NSA_BLOCK) == blk_t
    valid_t = (blk_t + 1) * NSA_BLOCK <= qpos_t + 1
    sel = jnp.where(cur, FORCE_SCORE, jnp.where(valid_t, imp, NEG_INF))
    blk_col = lax.broadcasted_iota(I32, (nb, 1), 0)
    rank = jnp.zeros((nb, tq), I32)
    for i in range(nb):
        row = sel[i:i + 1, :]
        rank = rank + jnp.where(row > sel, 1, jnp.where(row == sel, jnp.where(blk_col > i, 1, 0), 0))
    chosen_t = jnp.where((rank < NSA_TOPK) & (sel > NEG_INF / 2), 1.0, 0.0).astype(BF16)
    eye = jnp.where(lax.broadcasted_iota(I32, (tq, tq), 0) == lax.broadcasted_iota(I32, (tq, tq), 1), 1.0, 0.0)
    return _dot_nt(eye.astype(BF16), chosen_t).astype(BF16)


def _nsa_compressed_branch(qc, kcvc, pos0, tq, transposed_select=False):
    nb = kcvc.shape[0]
    rows = B_HEADS * tq
    lane = lax.broadcasted_iota(I32, kcvc.shape, 1)
    kc = jnp.where(lane < B_HD, kcvc, 0.0).astype(BF16)
    vc = jnp.where(lane >= B_HD, kcvc, 0.0).astype(BF16)
    sc = _dot_nt(qc, kc)
    qpos = pos0 + (lax.broadcasted_iota(I32, (rows, nb), 0) & (tq - 1))
    blk = lax.broadcasted_iota(I32, (rows, nb), 1)
    valid = (blk + 1) * NSA_BLOCK <= qpos + 1
    scm = jnp.where(valid, sc, NEG_INF)
    e = jnp.exp(scm - jnp.max(scm, axis=-1, keepdims=True))
    pc = e / jnp.sum(e, axis=-1, keepdims=True)
    pc = pc * jnp.where(qpos + 1 >= NSA_BLOCK, 1.0, 0.0)
    o_c = _dot(pc.astype(BF16), vc)
    if transposed_select:
        return o_c, _select_blocks_transposed(qc, kc, pos0, tq)
    imp = jnp.sum(pc.reshape(B_HEADS, tq, nb), axis=0)
    qpos_t = pos0 + lax.broadcasted_iota(I32, (tq, nb), 0)
    blk_t = lax.broadcasted_iota(I32, (tq, nb), 1)
    cur = (qpos_t // NSA_BLOCK) == blk_t
    valid_t = (blk_t + 1) * NSA_BLOCK <= qpos_t + 1
    sel = jnp.where(cur, FORCE_SCORE, jnp.where(valid_t, imp, NEG_INF))
    blk_row = lax.broadcasted_iota(I32, (1, nb), 1)
    rank = jnp.zeros((tq, nb), I32)
    for i in range(nb):
        col = sel[:, i:i + 1]
        rank = rank + jnp.where(col > sel, 1, jnp.where(col == sel, jnp.where(blk_row > i, 1, 0), 0))
    chosen = (rank < NSA_TOPK) & (sel > NEG_INF / 2)
    return o_c, jnp.where(chosen, 1.0, 0.0).astype(BF16)


def _expand_blocks(chosen, first_tok, width):
    nb = chosen.shape[1]
    tok = first_tok + lax.broadcasted_iota(I32, (nb, width), 1)
    blk = lax.broadcasted_iota(I32, (nb, width), 0)
    e = jnp.where((tok // NSA_BLOCK) == blk, 1.0, 0.0).astype(BF16)
    return _dot(chosen, e)


def _nsa_combine(g, o_c, o_s, o_w, tq):
    gate = jax.nn.sigmoid(g)
    lane = lax.broadcasted_iota(I32, (tq, LANE), 1)
    comb = []
    for h in range(B_HEADS):
        sl = slice(h * tq, (h + 1) * tq)
        comb.append(gate[:, 3 * h:3 * h + 1] * o_c[sl] + gate[:, 3 * h + 1:3 * h + 2] * o_s[sl]
                    + gate[:, 3 * h + 2:3 * h + 3] * o_w[sl])
    groups = [jnp.where(lane < B_HD, pltpu.roll(comb[2 * j], B_HD, 1), comb[2 * j + 1]) for j in range(2)]
    return jnp.concatenate(groups, axis=1)


def _nsa_prompt_kernel(qbp_ref, qbrp_ref, kcvc_ref, kvs_ref, kvw_ref, g_ref, o_ref, *, tq, tk):
    qi = pl.program_id(1)
    pos0 = qi * tq
    rows = B_HEADS * tq
    qc = jnp.concatenate([qbp_ref[:, h * LANE:(h + 1) * LANE] for h in range(B_HEADS)], axis=0)
    o_c, chosen = _nsa_compressed_branch(qc, kcvc_ref[...], pos0, tq, transposed_select=(tq % LANE == 0))
    qr = jnp.concatenate([qbrp_ref[:, h * LANE:(h + 1) * LANE] for h in range(B_HEADS)], axis=0)

    def masked(s, ok):
        w = s.shape[1]
        return jnp.where(ok[None], s.reshape(B_HEADS, tq, w), NEG_INF).reshape(rows, w)

    def values(kv):
        lane = lax.broadcasted_iota(I32, kv.shape, 1)
        return jnp.where(lane < B_HD, jnp.ones_like(kv), kv)

    qpos_s = pos0 + lax.broadcasted_iota(I32, (tq, tk), 0)
    c_s = lax.broadcasted_iota(I32, (tq, tk), 1)

    def sel_step(j, carry):
        m, acc = carry
        start = pl.multiple_of(j * tk, tk)
        kv = kvs_ref[pl.ds(start, tk), :]
        ok = (_expand_blocks(chosen, start, tk) > 0.5) & (start + c_s <= qpos_s)
        s = masked(_dot_nt(qr, kv), ok)
        m_new = jnp.maximum(m, jnp.max(s, axis=-1, keepdims=True))
        p = jnp.exp(s - m_new)
        acc = jnp.exp(m - m_new) * acc + _dot(p.astype(BF16), values(kv))
        return m_new, acc

    n_chunks = (pos0 + tq - 1) // tk + 1
    init = (jnp.full((rows, 1), NEG_INF, F32), jnp.zeros((rows, LANE), F32))
    _, acc = lax.fori_loop(0, n_chunks, sel_step, init)
    o_s = acc / acc[:, 0:1]

    span = WINDOW + tq
    start = pl.multiple_of(jnp.maximum(pos0 - WINDOW, 0), tq)
    kv = kvw_ref[pl.ds(start, span), :]
    qpos_w = pos0 + lax.broadcasted_iota(I32, (tq, span), 0)
    tok_w = start + lax.broadcasted_iota(I32, (tq, span), 1)
    s = masked(_dot_nt(qr, kv), (tok_w <= qpos_w) & (tok_w > qpos_w - WINDOW))
    p = jnp.exp(s - jnp.max(s, axis=-1, keepdims=True))
    acc = _dot(p.astype(BF16), values(kv))
    o_w = acc / acc[:, 0:1]

    o_ref[...] = _nsa_combine(g_ref[...], o_c, o_s, o_w, tq).astype(BF16)


def _nsa_prompt(qbp, qbrp, kcvc, kvsb, kvwb, g, nbatch, seq, tq, tk):
    nq = seq // tq
    nb = seq // NSA_BLOCK
    assert tk % tq == 0 and seq % tk == 0 and seq >= WINDOW + tq and WINDOW % tq == 0
    tile = lambda w: pl.BlockSpec((tq, w), lambda b, i: (b * nq + i, 0))
    return pl.pallas_call(
        functools.partial(_nsa_prompt_kernel, tq=tq, tk=tk),
        grid=(nbatch, nq),
        in_specs=[tile(4 * LANE), tile(4 * LANE), pl.BlockSpec((None, nb, LANE), lambda b, i: (b, 0, 0)),
                  pl.BlockSpec((seq, LANE), lambda b, i: (b, 0)), pl.BlockSpec((seq, LANE), lambda b, i: (b, 0)),
                  tile(LANE)],
        out_specs=tile(B_W),
        out_shape=jax.ShapeDtypeStruct((nbatch * seq, B_W), BF16),
        compiler_params=_params("parallel", "arbitrary"),
        name="nsa_prompt",
    )(qbp, qbrp, kcvc, kvsb, kvwb, g)


def _nsa_sample_body(qbp_ref, qbrp_ref, g_ref, cn_ref, sn_ref, wn_ref, win_ref, pe_ref, w_ref, cmp_pages, slc_pages,
                     o_ref, msum_ref, *, n_new, past, nbp):
    n_pages = len(cmp_pages)
    tq = TOK_PAD
    rows = B_HEADS * tq
    page_rows = cmp_pages[0].shape[0]
    bpp = page_rows // NSA_BLOCK
    nb_past = n_pages * bpp

    for p, pg in enumerate(cmp_pages):
        msum_ref[p * bpp:(p + 1) * bpp, :] = jnp.sum(pg[...].reshape(bpp, NSA_BLOCK, LANE), axis=1)
    msum_ref[nb_past:nb_past + 1, :] = jnp.sum(cn_ref[...], axis=0, keepdims=True)
    msum_ref[nb_past + 1:nbp, :] = jnp.zeros((nbp - nb_past - 1, LANE), F32)
    mblk = msum_ref[...] * (1.0 / NSA_BLOCK) + jnp.mean(pe_ref[...], axis=0, keepdims=True)
    kcvc = jnp.dot(mblk, w_ref[...], preferred_element_type=F32, precision=lax.Precision.HIGHEST)

    qc = jnp.concatenate([qbp_ref[:, h * LANE:(h + 1) * LANE] for h in range(B_HEADS)], axis=0)
    o_c, chosen = _nsa_compressed_branch(qc, kcvc, past, tq)
    qr = jnp.concatenate([qbrp_ref[:, h * LANE:(h + 1) * LANE] for h in range(B_HEADS)], axis=0)

    def masked(s, ok):
        w = s.shape[1]
        return jnp.where(ok[None], s.reshape(B_HEADS, tq, w), NEG_INF).reshape(rows, w)

    tok_n = lax.broadcasted_iota(I32, (tq, NEW_ROWS), 0)
    j_n = lax.broadcasted_iota(I32, (tq, NEW_ROWS), 1)
    new_ok = (j_n <= tok_n) & (j_n < n_new)

    def attend(score_tiles, value_tiles):
        m = None
        for s in score_tiles:
            mx = jnp.max(s, axis=-1, keepdims=True)
            m = mx if m is None else jnp.maximum(m, mx)
        l = jnp.zeros((rows, 1), F32)
        acc = jnp.zeros((rows, LANE), F32)
        for s, v in zip(score_tiles, value_tiles):
            p = jnp.exp(s - m)
            l = l + jnp.sum(p, axis=-1, keepdims=True)
            acc = acc + _dot(p.astype(BF16), v)
        return acc / l

    tiles, vals = [], []
    for p, pg in enumerate(slc_pages):
        kv = pg[...].astype(BF16)
        ok = _expand_blocks(chosen, p * page_rows, page_rows) > 0.5
        tiles.append(masked(_dot_nt(qr, kv), ok))
        vals.append(kv)
    kv = sn_ref[...].astype(BF16)
    ok = (_expand_blocks(chosen, past, NEW_ROWS) > 0.5) & new_ok
    tiles.append(masked(_dot_nt(qr, kv), ok))
    vals.append(kv)
    o_s = attend(tiles, vals)

    wb = win_ref.shape[0]
    kv = win_ref[...].astype(BF16)
    r_w = lax.broadcasted_iota(I32, (tq, wb), 1)
    t_w = lax.broadcasted_iota(I32, (tq, wb), 0)
    tiles = [masked(_dot_nt(qr, kv), r_w > t_w + (wb - WINDOW))]
    vals = [kv]
    kv = wn_ref[...].astype(BF16)
    tiles.append(masked(_dot_nt(qr, kv), new_ok))
    vals.append(kv)
    o_w = attend(tiles, vals)

    o_ref[...] = _nsa_combine(g_ref[...], o_c, o_s, o_w, tq).astype(BF16)


N_DIFF_SEQ_IN = 5
N_NSA_SEQ_IN = 9


def _sample_mixers_kernel(pt_ref, *refs, n_pages, n_new, past, nbp, lam_init):
    del pt_ref
    diff_in = refs[:N_DIFF_SEQ_IN]
    nsa_in = refs[N_DIFF_SEQ_IN:N_DIFF_SEQ_IN + N_NSA_SEQ_IN]
    pages = refs[N_DIFF_SEQ_IN + N_NSA_SEQ_IN:]
    k_pages, v_pages, cmp_pages, slc_pages = (pages[i * n_pages:(i + 1) * n_pages] for i in range(4))
    oa_ref, ob_ref, msum_ref = pages[4 * n_pages:]
    _diff_sample_body(*diff_in, k_pages, v_pages, oa_ref, n_new=n_new, lam_init=lam_init)
    _nsa_sample_body(*nsa_in, cmp_pages, slc_pages, ob_ref, msum_ref, n_new=n_new, past=past, nbp=nbp)


def _sample_mixers(pt_flat, lw, subw, q8, kn, vn, qbp8, qbrp8, g8, cn, sn, wn, state_win, pe_cat, w_bd, cache_k,
                   cache_v, cache_cmp, cache_slc, layer, lam_init, n_pages, n_new, past):
    nseq = q8.shape[0]
    new_rows = n_new * A_HEADS
    assert new_rows % 8 == 0
    wb = state_win.shape[2]
    nb = -(-(past + n_new) // NSA_BLOCK)
    nbp = -(-nb // 16) * 16

    def page_specs(cache):
        rows = cache.shape[2]
        return [pl.BlockSpec((None, None, rows, LANE), lambda b, pt, p=p: (layer, pt[b * n_pages + p], 0, 0))
                for p in range(n_pages)]

    seq_spec = lambda r, w: pl.BlockSpec((None, r, w), lambda b, pt: (b, 0, 0))
    const_spec = lambda r, w: pl.BlockSpec((r, w), lambda b, pt: (0, 0))
    caches = (cache_k, cache_v, cache_cmp, cache_slc)
    grid_spec = pltpu.PrefetchScalarGridSpec(
        num_scalar_prefetch=1,
        grid=(nseq,),
        in_specs=[const_spec(4, A_HD), const_spec(1, LANE), seq_spec(TOK_PAD, A_W), seq_spec(new_rows, LANE),
                  seq_spec(new_rows, LANE),
                  seq_spec(TOK_PAD, 4 * LANE), seq_spec(TOK_PAD, 4 * LANE), seq_spec(TOK_PAD, LANE),
                  seq_spec(NEW_ROWS, LANE), seq_spec(NEW_ROWS, LANE), seq_spec(NEW_ROWS, LANE),
                  pl.BlockSpec((None, None, wb, LANE), lambda b, pt: (layer, b, 0, 0)),
                  const_spec(NSA_BLOCK, LANE), const_spec(LANE, LANE)]
        + [s for c in caches for s in page_specs(c)],
        out_specs=[seq_spec(TOK_PAD, A_W), seq_spec(TOK_PAD, B_W)],
        scratch_shapes=[pltpu.VMEM((nbp, LANE), F32)],
    )
    return pl.pallas_call(
        functools.partial(_sample_mixers_kernel, n_pages=n_pages, n_new=n_new, past=past, nbp=nbp,
                          lam_init=lam_init),
        grid_spec=grid_spec,
        out_shape=[jax.ShapeDtypeStruct((nseq, TOK_PAD, A_W), BF16), jax.ShapeDtypeStruct((nseq, TOK_PAD, B_W), BF16)],
        compiler_params=_params("parallel"),
        name="sample_mixers",
    )(pt_flat, lw, subw, q8, kn, vn, qbp8, qbrp8, g8, cn, sn, wn, state_win, pe_cat, w_bd,
      *[c for c in caches for _ in range(n_pages)])


def _conv_prompt_kernel(c_ref, halo_ref, b_ref, w_ref, o_ref):
    i = pl.program_id(1)
    c = c_ref[...]
    rows = c.shape[0]
    halo = jnp.where(i > 0, halo_ref[...], 0.0)
    r = lax.broadcasted_iota(I32, c.shape, 0)
    c1 = jnp.where(r == 0, halo[7:8], pltpu.roll(c, 1, 0))
    c2 = jnp.where(r == 0, halo[6:7], jnp.where(r == 1, halo[7:8], pltpu.roll(c, 2, 0)))
    del rows
    z = w_ref[0:1] * c2 + w_ref[1:2] * c1 + w_ref[2:3] * c
    o_ref[...] = (b_ref[...] * z).astype(BF16)


def _conv_prompt(cin, bgate, conv_w, nbatch, seq, tm):
    nt = seq // tm
    hb = tm // 8
    tile = pl.BlockSpec((tm, C_W), lambda b, i: (b * nt + i, 0))
    return pl.pallas_call(
        _conv_prompt_kernel,
        grid=(nbatch, nt),
        in_specs=[tile, pl.BlockSpec((8, C_W), lambda b, i: (jnp.maximum((b * nt + i) * hb - 1, 0), 0)), tile,
                  pl.BlockSpec((CONV_W, C_W), lambda b, i: (0, 0))],
        out_specs=tile,
        out_shape=jax.ShapeDtypeStruct((nbatch * seq, C_W), BF16),
        compiler_params=_params("parallel", "parallel"),
        name="conv_prompt",
    )(cin, cin, bgate, conv_w)


def _conv_sample_kernel(c0_ref, c1_ref, c2_ref, b_ref, w_ref, o_ref):
    z = w_ref[0:1] * c0_ref[...] + w_ref[1:2] * c1_ref[...] + w_ref[2:3] * c2_ref[...]
    o_ref[...] = (b_ref[...] * z).astype(BF16)


def _conv_sample(c0, c1, c2, bgate, conv_w):
    return pl.pallas_call(
        _conv_sample_kernel,
        out_shape=jax.ShapeDtypeStruct(c0.shape, BF16),
        name="conv_sample",
    )(c0, c1, c2, bgate, conv_w)


def _mix_out_kernel(x_ref, oap_ref, obp_ref, ocp_ref, oas_ref, obs_ref, ocs_ref, wout_ref, nw_ref, wq_ref, x_out_ref,
                    q_out_ref, *, n_ptiles):
    is_prompt = pl.program_id(0) < n_ptiles
    y = (_dot(jnp.where(is_prompt, oap_ref[...], oas_ref[...]), wout_ref[0:A_W])
         + _dot(jnp.where(is_prompt, obp_ref[...], obs_ref[...]), wout_ref[A_W:A_W + B_W])
         + _dot(jnp.where(is_prompt, ocp_ref[...], ocs_ref[...]), wout_ref[A_W + B_W:A_W + B_W + C_W]))
    x = x_ref[...] + _rms(y, nw_ref[NORM_MIX_POST:NORM_MIX_POST + 1])
    x_out_ref[...] = x
    h = _rms(x, nw_ref[NORM_X_PRE:NORM_X_PRE + 1]).astype(BF16)
    q_out_ref[...] = (_dot(h, wq_ref[...]) * Q_SCALE).astype(BF16)


def _mix_out(x, mixed_p, mixed_s, w_out, nw8, w_q, tm):
    n, d = x.shape
    n_ptiles = mixed_p[0].shape[0] // tm
    assert mixed_p[0].shape[0] % tm == 0 and mixed_s[0].shape[0] % tm == 0
    row = lambda w: pl.BlockSpec((tm, w), lambda i: (i, 0))
    prow = lambda a: pl.BlockSpec((tm, a.shape[1]), lambda i: (jnp.minimum(i, n_ptiles - 1), 0))
    srow = lambda a: pl.BlockSpec((tm, a.shape[1]), lambda i: (jnp.maximum(i - n_ptiles, 0), 0))
    full = lambda a: pl.BlockSpec(a.shape, lambda i: (0, 0))
    return pl.pallas_call(
        functools.partial(_mix_out_kernel, n_ptiles=n_ptiles),
        grid=(n // tm,),
        in_specs=[row(d)] + [prow(a) for a in mixed_p] + [srow(a) for a in mixed_s]
        + [full(w_out), full(nw8), full(w_q)],
        out_specs=[row(d), row(X_W)],
        out_shape=[jax.ShapeDtypeStruct((n, d), F32), jax.ShapeDtypeStruct((n, X_W), BF16)],
        compiler_params=_params("parallel"),
        name="mix_out",
    )(x, *mixed_p, *mixed_s, w_out, nw8, w_q)


def _cross_attn_kernel(q_ref, mkt_ref, mvt_ref, o_ref):
    q = q_ref[...]
    ts = q.shape[0]
    lane = lax.broadcasted_iota(I32, q.shape, 1)
    zero = jnp.zeros_like(q)
    qs = jnp.concatenate(
        [jnp.where((lane >= h * X_HD) & (lane < (h + 1) * X_HD), q, zero) for h in range(X_HEADS)], axis=0)
    s = _dot(qs, mkt_ref[...].astype(BF16))
    e = jnp.exp(s - jnp.max(s, axis=-1, keepdims=True))
    p = e / jnp.sum(e, axis=-1, keepdims=True)
    o = _dot_nt(p.astype(BF16), mvt_ref[...].astype(BF16))
    out = jnp.zeros((ts, X_W), F32)
    for h in range(X_HEADS):
        out = out + jnp.where((lane >= h * X_HD) & (lane < (h + 1) * X_HD), o[h * ts:(h + 1) * ts], 0.0)
    o_ref[...] = out.astype(BF16)


def _cross_attn(q, mkt, mvt, ts, v_block=0, b_off=0):
    nbatch, seq, _ = q.shape
    n_mem = mkt.shape[2]
    return pl.pallas_call(
        _cross_attn_kernel,
        grid=(nbatch, seq // ts),
        in_specs=[pl.BlockSpec((None, ts, X_W), lambda b, i: (b, i, 0)),
                  pl.BlockSpec((None, X_W, n_mem), lambda b, i: (b_off + b, 0, 0)),
                  pl.BlockSpec((None, X_W, n_mem), lambda b, i: (b_off + b, v_block, 0))],
        out_specs=pl.BlockSpec((None, ts, X_W), lambda b, i: (b, i, 0)),
        out_shape=jax.ShapeDtypeStruct((nbatch, seq, X_W), BF16),
        compiler_params=_params("parallel", "parallel"),
        name="cross_attn",
    )(q, mkt, mvt)


def _mem_kv_kernel(m_ref, nw_ref, wt_ref, o_ref):
    h = _rms(m_ref[...], nw_ref[...]).astype(BF16)
    o_ref[...] = _dot_nt(wt_ref[...], h)


def _mem_kv(mem, nw, w_kv_t):
    nbatch, n_mem, d = mem.shape
    return pl.pallas_call(
        _mem_kv_kernel,
        grid=(nbatch,),
        in_specs=[pl.BlockSpec((None, n_mem, d), lambda b: (b, 0, 0)), pl.BlockSpec((1, d), lambda b: (0, 0)),
                  pl.BlockSpec(w_kv_t.shape, lambda b: (0, 0))],
        out_specs=pl.BlockSpec((None, 2 * X_W, n_mem), lambda b: (b, 0, 0)),
        out_shape=jax.ShapeDtypeStruct((nbatch, 2 * X_W, n_mem), F32),
        compiler_params=_params("parallel"),
        name="mem_kv",
    )(mem, nw, w_kv_t)


def _ffn_kernel(x_ref, o_ref, wo_ref, nw_ref, wg_ref, wu_ref, wd_ref, y_ref, x2_ref, h_ref, acc_ref):
    j = pl.program_id(1)

    @pl.when(j == 0)
    def _():
        x2 = x_ref[...] + _rms(_dot(o_ref[...], wo_ref[...]), nw_ref[NORM_X_POST:NORM_X_POST + 1])
        x2_ref[...] = x2
        h_ref[...] = _rms(x2, nw_ref[NORM_FFN_PRE:NORM_FFN_PRE + 1]).astype(BF16)
        acc_ref[...] = jnp.zeros_like(acc_ref)

    h = h_ref[...]
    g = _dot(h, wg_ref[...])
    u = _dot(h, wu_ref[...])
    a = (g * jax.nn.sigmoid(g)) * u
    acc_ref[...] += _dot(a.astype(BF16), wd_ref[...])

    @pl.when(j == pl.num_programs(1) - 1)
    def _():
        y_ref[...] = x2_ref[...] + _rms(acc_ref[...], nw_ref[NORM_FFN_POST:NORM_FFN_POST + 1])


def _ffn(x, o, w_o, nw8, w_g, w_u, w_d, tm, tf):
    n, d = x.shape
    dff = w_g.shape[1]
    row = lambda w: pl.BlockSpec((tm, w), lambda i, j: (i, 0))
    return pl.pallas_call(
        _ffn_kernel,
        grid=(n // tm, dff // tf),
        in_specs=[row(d), row(X_W), pl.BlockSpec(w_o.shape, lambda i, j: (0, 0)),
                  pl.BlockSpec(nw8.shape, lambda i, j: (0, 0)),
                  pl.BlockSpec((d, tf), lambda i, j: (0, j)), pl.BlockSpec((d, tf), lambda i, j: (0, j)),
                  pl.BlockSpec((tf, d), lambda i, j: (j, 0))],
        out_specs=row(d),
        out_shape=jax.ShapeDtypeStruct((n, d), F32),
        scratch_shapes=[pltpu.VMEM((tm, d), F32), pltpu.VMEM((tm, d), BF16), pltpu.VMEM((tm, d), F32)],
        compiler_params=_params("parallel", "arbitrary"),
        name="ffn",
    )(x, o, w_o, nw8, w_g, w_u, w_d)


def _rope_tables(pos):
    inv = ROPE_THETA ** (-jnp.arange(ROPE_HALF, dtype=F32) / ROPE_HALF)
    ang = pos.astype(F32)[:, None] * inv[None, :]
    cos, sin = jnp.cos(ang), jnp.sin(ang)
    n = pos.shape[0]
    rest = A_HD - 2 * ROPE_HALF
    z8 = jnp.zeros((n, ROPE_HALF), F32)
    c = jnp.concatenate([cos, cos, jnp.ones((n, rest), F32)], axis=1)
    s1 = jnp.concatenate([-sin, z8, jnp.zeros((n, rest), F32)], axis=1)
    s2 = jnp.concatenate([z8, sin, jnp.zeros((n, rest), F32)], axis=1)
    return jnp.concatenate([c, c, s1, s1, s2, s2], axis=1)


def _pad_rows(t, rows):
    return jnp.pad(t, ((0, 0), (0, rows - t.shape[1]), (0, 0)))


def kernel(x_prompt, x_sample, cache_diff_k, cache_diff_v, cache_nsa_cmp, cache_nsa_slc, state_nsa_win, state_conv,
           cache_mem_k, cache_mem_v, page_table, mem_prompt, norm_w, w_in, w_out, diff_lambda, diff_subln, nsa_pe,
           nsa_w_cmp, conv_w, w_q_mem, w_kv_mem, w_o_mem, w_ffn_up, w_ffn_down):
    depth = w_in.shape[0]
    nbp, seq, d = x_prompt.shape
    nbs, sseq, _ = x_sample.shape
    n_p, n_s = nbp * seq, nbs * sseq
    n_pool, page_rows = cache_diff_k.shape[1], cache_diff_k.shape[2]
    n_pages = page_table.shape[1]
    past = n_pages * page_rows
    n_mem = mem_prompt.shape[1]
    dff = w_ffn_down.shape[1]
    assert seq % 256 == 0 and seq >= WINDOW and sseq <= TOK_PAD and d == 1024

    tm = _pick_tile(math.gcd(n_p, n_s), 512)
    tf = _pick_tile(dff, 1408, 128)

    split = 2176
    w_in_p = jnp.concatenate(
        [w_in[:, :, :split], w_in[:, :, split + 3 * B_HEADS:], w_in[:, :, split:split + 3 * B_HEADS],
         jnp.zeros((depth, d, IN_COLS_PADDED - w_in.shape[2]), w_in.dtype)], axis=2).astype(BF16)
    w_out_b = w_out.astype(BF16)
    w_q_b = w_q_mem.astype(BF16)
    w_kv_t = jnp.swapaxes(w_kv_mem, 1, 2).astype(BF16)
    mkt_s, mvt_s = (jnp.transpose(t, (0, 1, 3, 4, 2)).reshape(depth * nbs, X_W, n_mem)
                    for t in (cache_mem_k, cache_mem_v))
    w_o_b = w_o_mem.astype(BF16)
    w_g_b = w_ffn_up[:, :, :dff].astype(BF16)
    w_u_b = w_ffn_up[:, :, dff:].astype(BF16)
    w_d_b = w_ffn_down.astype(BF16)
    nw8 = jnp.pad(norm_w, ((0, 0), (0, 1), (0, 0)))
    pe_cat = jnp.concatenate([nsa_pe[:, 0], nsa_pe[:, 1]], axis=-1)
    zb = jnp.zeros((depth, B_HD, B_HD), F32)
    w_bd = jnp.concatenate([jnp.concatenate([nsa_w_cmp[:, 0], zb], axis=2),
                            jnp.concatenate([zb, nsa_w_cmp[:, 1]], axis=2)], axis=1)
    subw = diff_subln.reshape(depth, 1, 2 * A_HD)

    pos = jnp.concatenate([jnp.tile(jnp.arange(seq, dtype=I32), nbp),
                           jnp.tile(past + jnp.arange(sseq, dtype=I32), nbs)])
    rope = _rope_tables(pos)
    pt_flat = page_table.reshape(-1).astype(I32)
    ck = cache_diff_k.reshape(depth, n_pool, page_rows * A_HEADS, 2 * A_HD)
    cv = cache_diff_v.reshape(depth, n_pool, page_rows * A_HEADS, 2 * A_HD)

    x = jnp.concatenate([x_prompt.reshape(n_p, d), x_sample.reshape(n_s, d)], axis=0)
    outs = [[] for _ in range(14)]

    def sample3(t, rows):
        return _pad_rows(t[n_p:].reshape(nbs, sseq, t.shape[1]), rows)

    for l in range(depth):
        lam_init = 0.8 - 0.6 * math.exp(-0.3 * l)
        (qa, ka, kab, va, vab, qbp, qbrp, kvc, kvs, kvsb, kvw, kvwb, g, cin, bgate) = _proj_in(
            x, norm_w[l, NORM_MIX_PRE:NORM_MIX_PRE + 1], w_in_p[l], rope, tm)

        oa_p = _diff_prompt(qa, kab, vab, diff_lambda[l], subw[l], lam_init, nbp, seq, 512, 512)
        kcvc = _nsa_compress(kvc, pe_cat[l], w_bd[l], nbp, seq)
        ob_p = _nsa_prompt(qbp, qbrp, kcvc, kvsb, kvwb, g, nbp, seq, 128, 512)
        oc_p = _conv_prompt(cin, bgate, conv_w[l], nbp, seq, _pick_tile(seq, 512))

        new_heads = lambda t: t[n_p:].reshape(nbs, sseq * A_HEADS, 2 * A_HD)
        oa_s, ob_s = _sample_mixers(
            pt_flat, diff_lambda[l], subw[l], sample3(qa, TOK_PAD), new_heads(ka), new_heads(va),
            sample3(qbp, TOK_PAD), sample3(qbrp, TOK_PAD), sample3(g, TOK_PAD), sample3(kvc, NEW_ROWS),
            sample3(kvs, NEW_ROWS), sample3(kvw, NEW_ROWS), state_nsa_win, pe_cat[l], w_bd[l], ck, cv, cache_nsa_cmp,
            cache_nsa_slc, l, lam_init, n_pages, sseq, past)
        c_all = jnp.concatenate([state_conv[l], cin[n_p:].reshape(nbs, sseq, C_W)], axis=1)
        oc_s = _conv_sample(*[c_all[:, j:j + sseq].reshape(n_s, C_W) for j in range(CONV_W)], bgate[n_p:],
                            conv_w[l])

        x, qx = _mix_out(x, (oa_p, ob_p, oc_p),
                         (oa_s[:, :sseq].reshape(n_s, A_W), ob_s[:, :sseq].reshape(n_s, B_W), oc_s),
                         w_out_b[l], nw8[l], w_q_b[l], tm)

        mkv_t = _mem_kv(mem_prompt, norm_w[l, NORM_MEM:NORM_MEM + 1], w_kv_t[l])
        ox_p = _cross_attn(qx[:n_p].reshape(nbp, seq, X_W), mkv_t, mkv_t, _pick_tile(seq, 512), v_block=1)
        ox_s = _cross_attn(_pad_rows(qx[n_p:].reshape(nbs, sseq, X_W), TOK_PAD), mkt_s, mvt_s, TOK_PAD,
                           b_off=l * nbs)
        mk_p, mv_p = (jnp.transpose(t.reshape(nbp, X_HEADS, X_HD, n_mem), (0, 3, 1, 2))
                      for t in (mkv_t[:, :X_W], mkv_t[:, X_W:]))
        ox = jnp.concatenate([ox_p.reshape(n_p, X_W), ox_s[:, :sseq].reshape(n_s, X_W)], axis=0)
        x = _ffn(x, ox, w_o_b[l], nw8[l], w_g_b[l], w_u_b[l], w_d_b[l], tm, tf)

        win_p = kvw[:n_p].reshape(nbp, seq, 2 * B_HD)[:, seq - WINDOW:]
        cin_p = cin[:n_p].reshape(nbp, seq, C_W)
        win_s = jnp.concatenate([state_nsa_win[l], kvw[n_p:].reshape(nbs, sseq, 2 * B_HD)], axis=1)[:, sseq:]
        layer_out = (
            ka[:n_p].reshape(nbp, seq, A_HEADS, 2 * A_HD), va[:n_p].reshape(nbp, seq, A_HEADS, 2 * A_HD),
            kvc[:n_p].reshape(nbp, seq, 2 * B_HD), kvs[:n_p].reshape(nbp, seq, 2 * B_HD), win_p,
            cin_p[:, seq - (CONV_W - 1):], mk_p, mv_p,
            ka[n_p:].reshape(nbs, sseq, A_HEADS, 2 * A_HD), va[n_p:].reshape(nbs, sseq, A_HEADS, 2 * A_HD),
            kvc[n_p:].reshape(nbs, sseq, 2 * B_HD), kvs[n_p:].reshape(nbs, sseq, 2 * B_HD), win_s, c_all[:, sseq:])
        for acc, t in zip(outs, layer_out):
            acc.append(t)

    stacked = [jnp.stack(t) for t in outs]
    return (x[:n_p].reshape(nbp, seq, d), x[n_p:].reshape(nbs, sseq, d), *stacked)
```

```python
import functools
import math

import jax
import jax.numpy as jnp
from jax import lax
from jax.experimental import pallas as pl
from jax.experimental.pallas import tpu as pltpu

F32 = jnp.float32
BF16 = jnp.bfloat16
I32 = jnp.int32

A_HEADS = 4
A_HD = 64
A_W = A_HEADS * 2 * A_HD
B_HEADS = 4
B_HD = 64
B_W = B_HEADS * B_HD
NSA_BLOCK = 64
NSA_TOPK = 16
WINDOW = 512
C_W = 256
CONV_W = 3
X_HEADS = 4
X_HD = 64
X_W = X_HEADS * X_HD
ROPE_THETA = 500000.0
ROPE_HALF = (A_HD // 4) // 2
EPS = 1e-6
NEG_INF = -1e30
FORCE_SCORE = 1e4
Q_SCALE = A_HD ** -0.5 * math.log2(math.e)
LANE = 128
IN_COLS_PADDED = 3072
NEW_ROWS = 16
TOK_PAD = 8
VMEM_LIMIT = 56 * 1024 * 1024

NORM_MIX_PRE, NORM_MIX_POST, NORM_X_PRE, NORM_X_POST, NORM_FFN_PRE, NORM_FFN_POST, NORM_MEM = range(7)

NT_DIMS = (((1,), (1,)), ((), ()))


def _params(*sem):
    return pltpu.CompilerParams(dimension_semantics=sem, vmem_limit_bytes=VMEM_LIMIT)


def _pick_tile(n, target, mult=16):
    best = None
    for t in range(mult, min(n, target) + 1, mult):
        if n % t == 0:
            best = t
    assert best is not None, (n, target)
    return best


def _rms(x, w):
    return x * lax.rsqrt(jnp.mean(x * x, axis=-1, keepdims=True) + EPS) * w


def _dot(a, b):
    return jnp.dot(a, b, preferred_element_type=F32)


def _dot_nt(a, b):
    return lax.dot_general(a, b, NT_DIMS, preferred_element_type=F32)


def _rope(z, c, s1, s2):
    outs = []
    for j in range(z.shape[1] // LANE):
        t = z[:, j * LANE:(j + 1) * LANE]
        outs.append(t * c + pltpu.roll(t, LANE - ROPE_HALF, 1) * s1 + pltpu.roll(t, ROPE_HALF, 1) * s2)
    return outs[0] if len(outs) == 1 else jnp.concatenate(outs, axis=1)


def _softmax_update(s, v, m, l, acc):
    m_new = jnp.maximum(m, jnp.max(s, axis=-1, keepdims=True))
    p = jnp.exp2(s - m_new)
    alpha = jnp.exp2(m - m_new)
    l = alpha * l + jnp.sum(p, axis=-1, keepdims=True)
    acc = alpha * acc + _dot(p.astype(BF16), v)
    return m_new, l, acc


def _proj_in_kernel(x_ref, nw_ref, w_ref, rope_ref, qa_ref, ka_ref, kab_ref, va_ref, vab_ref, qbp_ref, qbrp_ref,
                    kvc_ref, kvs_ref, kvsb_ref, kvw_ref, kvwb_ref, g_ref, cin_ref, bgate_ref):
    h = _rms(x_ref[...], nw_ref[...]).astype(BF16)
    c = rope_ref[:, 0:LANE]
    s1 = rope_ref[:, LANE:2 * LANE]
    s2 = rope_ref[:, 2 * LANE:3 * LANE]

    def mm(a, b):
        return _dot(h, w_ref[:, a:b])

    qa_ref[...] = (_rope(mm(0, 512), c, s1, s2) * Q_SCALE).astype(BF16)
    zk = _rope(mm(512, 1024), c, s1, s2)
    ka_ref[...] = zk
    kab_ref[...] = zk.astype(BF16)
    zv = mm(1024, 1536)
    va_ref[...] = zv
    vab_ref[...] = zv.astype(BF16)

    zb = mm(1536, 1792) * Q_SCALE
    zbr = _rope(zb, c, s1, s2)
    first = lax.broadcasted_iota(I32, (zb.shape[0], LANE), 1) < B_HD

    def pad_heads(z):
        outs = []
        for j in range(2):
            t = z[:, j * LANE:(j + 1) * LANE]
            outs.append(jnp.where(first, t, 0.0))
            outs.append(jnp.where(first, pltpu.roll(t, B_HD, 1), 0.0))
        return jnp.concatenate(outs, axis=1)

    qbp_ref[...] = pad_heads(zb).astype(BF16)
    qbrp_ref[...] = pad_heads(zbr).astype(BF16)

    zkv = mm(1792, 2176)
    kvc_ref[...] = zkv[:, 0:LANE]
    ckv = jnp.where(first, c, 1.0)
    s1kv = jnp.where(first, s1, 0.0)
    s2kv = jnp.where(first, s2, 0.0)
    ks = _rope(zkv[:, LANE:2 * LANE], ckv, s1kv, s2kv)
    kvs_ref[...] = ks
    kvsb_ref[...] = ks.astype(BF16)
    kw = _rope(zkv[:, 2 * LANE:3 * LANE], ckv, s1kv, s2kv)
    kvw_ref[...] = kw
    kvwb_ref[...] = kw.astype(BF16)

    zc = mm(2176, 2944)
    cin_ref[...] = zc[:, 2 * C_W:3 * C_W] * zc[:, 0:C_W]
    bgate_ref[...] = zc[:, C_W:2 * C_W]
    g_ref[...] = mm(2944, 3072)


def _proj_in(x, nw, w, rope, tm):
    n, d = x.shape
    period_tiles = rope.shape[0] // tm
    assert rope.shape[0] % tm == 0
    row = lambda wdt: pl.BlockSpec((tm, wdt), lambda i: (i, 0))
    rope_spec = pl.BlockSpec((tm, 3 * LANE), lambda i: (i % period_tiles, 0))
    widths = [(512, BF16), (512, F32), (512, BF16), (512, F32), (512, BF16), (512, BF16), (512, BF16),
              (128, F32), (128, F32), (128, BF16), (128, F32), (128, BF16), (128, F32), (C_W, F32), (C_W, F32)]
    return pl.pallas_call(
        _proj_in_kernel,
        grid=(n // tm,),
        in_specs=[row(d), pl.BlockSpec((1, d), lambda i: (0, 0)),
                  pl.BlockSpec((d, IN_COLS_PADDED), lambda i: (0, 0)), rope_spec],
        out_specs=[row(wd) for wd, _ in widths],
        out_shape=[jax.ShapeDtypeStruct((n, wd), dt) for wd, dt in widths],
        compiler_params=_params("parallel"),
        name="proj_in",
    )(x, nw, w, rope)


def _diff_lambda(lw):
    a = jnp.sum(lw[0:1] * lw[1:2], axis=-1, keepdims=True)
    b = jnp.sum(lw[2:3] * lw[3:4], axis=-1, keepdims=True)
    return jnp.exp(a) - jnp.exp(b)


def _diff_finish(o1, o2, lam, subw, one_minus_init):
    d = o1 - lam * o2
    return _rms(d, subw) * one_minus_init


def _diff_prompt_kernel(lw_ref, q_ref, k_ref, v_ref, subw_ref, o_ref, *, tq, tk, lam_init):
    qi = pl.program_id(2)
    q = q_ref[...]
    lane = lax.broadcasted_iota(I32, (tq, LANE), 1)
    zero = jnp.zeros_like(q)
    qs = jnp.concatenate([jnp.where(lane < A_HD, q, zero), jnp.where(lane >= A_HD, q, zero)], axis=0)
    rows = 2 * tq

    def step(j, carry, diag):
        m, l, acc = carry
        start = pl.multiple_of(j * tk, tk)
        s = _dot_nt(qs, k_ref[pl.ds(start, tk), :])
        if diag:
            r = lax.broadcasted_iota(I32, (rows, tk), 0) & (tq - 1)
            cidx = lax.broadcasted_iota(I32, (rows, tk), 1)
            s = jnp.where(start + cidx <= qi * tq + r, s, NEG_INF)
        return _softmax_update(s, v_ref[pl.ds(start, tk), :], m, l, acc)

    n_full = (qi * tq) // tk
    init = (jnp.full((rows, 1), NEG_INF, F32), jnp.zeros((rows, 1), F32), jnp.zeros((rows, LANE), F32))
    carry = lax.fori_loop(0, n_full, lambda j, cr: step(j, cr, False), init)
    m, l, acc = step(n_full, carry, True)
    o = acc / l
    lam = _diff_lambda(lw_ref[...]) + lam_init
    o_ref[...] = _diff_finish(o[:tq], o[tq:], lam, subw_ref[...], 1.0 - lam_init).astype(BF16)


def _diff_prompt(qa, kab, vab, lw, subw, lam_init, nbatch, seq, tq, tk):
    nq = seq // tq
    assert tk % tq == 0 and seq % tk == 0
    return pl.pallas_call(
        functools.partial(_diff_prompt_kernel, tq=tq, tk=tk, lam_init=lam_init),
        grid=(nbatch, A_HEADS, nq),
        in_specs=[pl.BlockSpec((4, A_HD), lambda b, h, i: (0, 0)),
                  pl.BlockSpec((tq, LANE), lambda b, h, i: (b * nq + i, h)),
                  pl.BlockSpec((seq, LANE), lambda b, h, i: (b, h)),
                  pl.BlockSpec((seq, LANE), lambda b, h, i: (b, h)),
                  pl.BlockSpec((1, LANE), lambda b, h, i: (0, 0))],
        out_specs=pl.BlockSpec((tq, LANE), lambda b, h, i: (b * nq + i, h)),
        out_shape=jax.ShapeDtypeStruct((nbatch * seq, A_W), BF16),
        compiler_params=_params("parallel", "parallel", "arbitrary"),
        name="diff_prompt",
    )(lw, qa, kab, vab, subw)


def _diff_sample_body(lw_ref, subw_ref, q_ref, kn_ref, vn_ref, k_pages, v_pages, o_ref, *, n_new, lam_init):
    q = q_ref[...]
    lane = lax.broadcasted_iota(I32, (TOK_PAD, LANE), 1)
    pieces = []
    for h in range(A_HEADS):
        qh = q[:, h * LANE:(h + 1) * LANE]
        pieces += [jnp.where(lane < A_HD, qh, jnp.zeros_like(qh)), jnp.where(lane >= A_HD, qh, jnp.zeros_like(qh))]
    qs = jnp.concatenate(pieces, axis=0)
    rows = qs.shape[0]
    rows_per_head = 2 * TOK_PAD

    def own_head(width):
        r = lax.broadcasted_iota(I32, (rows, width), 0)
        c = lax.broadcasted_iota(I32, (rows, width), 1)
        return (c & (A_HEADS - 1)) == (r // rows_per_head), r, c

    ok_page, _, _ = own_head(k_pages[0].shape[0])
    s_past = [jnp.where(ok_page, _dot_nt(qs, kp[...].astype(BF16)), NEG_INF) for kp in k_pages]
    ok_new, r_new, c_new = own_head(n_new * A_HEADS)
    ok_new = ok_new & ((c_new // A_HEADS) <= (r_new & (TOK_PAD - 1)))
    s_new = jnp.where(ok_new, _dot_nt(qs, kn_ref[...].astype(BF16)), NEG_INF)
    m = jnp.max(s_new, axis=-1, keepdims=True)
    for s in s_past:
        m = jnp.maximum(m, jnp.max(s, axis=-1, keepdims=True))
    p_new = jnp.exp2(s_new - m)
    l = jnp.sum(p_new, axis=-1, keepdims=True)
    acc = _dot(p_new.astype(BF16), vn_ref[...].astype(BF16))
    for s, vp in zip(s_past, v_pages):
        p = jnp.exp2(s - m)
        l = l + jnp.sum(p, axis=-1, keepdims=True)
        acc = acc + _dot(p.astype(BF16), vp[...].astype(BF16))
    o = acc / l
    lam = _diff_lambda(lw_ref[...]) + lam_init
    outs = []
    for h in range(A_HEADS):
        o1 = o[(2 * h) * TOK_PAD:(2 * h + 1) * TOK_PAD]
        o2 = o[(2 * h + 1) * TOK_PAD:(2 * h + 2) * TOK_PAD]
        outs.append(_diff_finish(o1, o2, lam, subw_ref[...], 1.0 - lam_init))
    o_ref[...] = jnp.concatenate(outs, axis=1).astype(BF16)


def _nsa_compress_kernel(kv_ref, pe_ref, w_ref, o_ref, *, nb):
    kv = kv_ref[...]
    m = jnp.sum(kv.reshape(nb, NSA_BLOCK, LANE), axis=1) * (1.0 / NSA_BLOCK)
    m = m + jnp.mean(pe_ref[...], axis=0, keepdims=True)
    o_ref[...] = jnp.dot(m, w_ref[...], preferred_element_type=F32, precision=lax.Precision.HIGHEST)


def _nsa_compress(kvc, pe_cat, w_bd, nbatch, seq):
    nb = seq // NSA_BLOCK
    return pl.pallas_call(
        functools.partial(_nsa_compress_kernel, nb=nb),
        grid=(nbatch,),
        in_specs=[pl.BlockSpec((seq, LANE), lambda b: (b, 0)), pl.BlockSpec((NSA_BLOCK, LANE), lambda b: (0, 0)),
                  pl.BlockSpec((LANE, LANE), lambda b: (0, 0))],
        out_specs=pl.BlockSpec((None, nb, LANE), lambda b: (b, 0, 0)),
        out_shape=jax.ShapeDtypeStruct((nbatch, nb, LANE), F32),
        compiler_params=_params("parallel"),
        name="nsa_compress",
    )(kvc, pe_cat, w_bd)


def _select_blocks_transposed(qc, kc, pos0, tq):
    nb = kc.shape[0]
    rows = B_HEADS * tq
    sc = _dot_nt(kc, qc)
    qpos = pos0 + (lax.broadcasted_iota(I32, (nb, rows), 1) & (tq - 1))
    blk = lax.broadcasted_iota(I32, (nb, rows), 0)
    scm = jnp.where((blk + 1) * NSA_BLOCK <= qpos + 1, sc, NEG_INF)
    e = jnp.exp2(scm - jnp.max(scm, axis=0, keepdims=True))
    pc = e / jnp.sum(e, axis=0, keepdims=True)
    pc = pc * jnp.where(qpos + 1 >= NSA_BLOCK, 1.0, 0.0)
    imp = pc[:, 0:tq]
    for h in range(1, B_HEADS):
        imp = imp + pc[:, h * tq:(h + 1) * tq]
    qpos_t = pos0 + lax.broadcasted_iota(I32, (nb, tq), 1)
    blk_t = lax.broadcasted_iota(I32, (nb, tq), 0)
    cur = (qpos_t // NSA_BLOCK) == blk_t
    valid_t = (blk_t + 1) * NSA_BLOCK <= qpos_t + 1
    sel = jnp.where(cur, FORCE_SCORE, jnp.where(valid_t, imp, NEG_INF))
    blk_col = lax.broadcasted_iota(I32, (nb, 1), 0)
    rank = jnp.zeros((nb, tq), I32)
    for i in range(nb):
        row = sel[i:i + 1, :]
        rank = rank + jnp.where(row > sel, 1, jnp.where(row == sel, jnp.where(blk_col > i, 1, 0), 0))
    chosen_t = jnp.where((rank < NSA_TOPK) & (sel > NEG_INF / 2), 1.0, 0.0).astype(BF16)
    eye = jnp.where(lax.broadcasted_iota(I32, (tq, tq), 0) == lax.broadcasted_iota(I32, (tq, tq), 1), 1.0, 0.0)
    return _dot_nt(eye.astype(BF16), chosen_t).astype(BF16)


def _nsa_compressed_branch(qc, kcvc, pos0, tq, transposed_select=False):
    nb = kcvc.shape[0]
    rows = B_HEADS * tq
    lane = lax.broadcasted_iota(I32, kcvc.shape, 1)
    kc = jnp.where(lane < B_HD, kcvc, 0.0).astype(BF16)
    vc = jnp.where(lane >= B_HD, kcvc, 0.0).astype(BF16)
    sc = _dot_nt(qc, kc)
    qpos = pos0 + (lax.broadcasted_iota(I32, (rows, nb), 0) & (tq - 1))
    blk = lax.broadcasted_iota(I32, (rows, nb), 1)
    valid = (blk + 1) * NSA_BLOCK <= qpos + 1
    scm = jnp.where(valid, sc, NEG_INF)
    e = jnp.exp2(scm - jnp.max(scm, axis=-1, keepdims=True))
    pc = e / jnp.sum(e, axis=-1, keepdims=True)
    pc = pc * jnp.where(qpos + 1 >= NSA_BLOCK, 1.0, 0.0)
    o_c = _dot(pc.astype(BF16), vc)
    if transposed_select:
        return o_c, _select_blocks_transposed(qc, kc, pos0, tq)
    imp = jnp.sum(pc.reshape(B_HEADS, tq, nb), axis=0)
    qpos_t = pos0 + lax.broadcasted_iota(I32, (tq, nb), 0)
    blk_t = lax.broadcasted_iota(I32, (tq, nb), 1)
    cur = (qpos_t // NSA_BLOCK) == blk_t
    valid_t = (blk_t + 1) * NSA_BLOCK <= qpos_t + 1
    sel = jnp.where(cur, FORCE_SCORE, jnp.where(valid_t, imp, NEG_INF))
    blk_row = lax.broadcasted_iota(I32, (1, nb), 1)
    rank = jnp.zeros((tq, nb), I32)
    for i in range(nb):
        col = sel[:, i:i + 1]
        rank = rank + jnp.where(col > sel, 1, jnp.where(col == sel, jnp.where(blk_row > i, 1, 0), 0))
    chosen = (rank < NSA_TOPK) & (sel > NEG_INF / 2)
    return o_c, jnp.where(chosen, 1.0, 0.0).astype(BF16)


def _expand_blocks(chosen, first_tok, width):
    nb = chosen.shape[1]
    tok = first_tok + lax.broadcasted_iota(I32, (nb, width), 1)
    blk = lax.broadcasted_iota(I32, (nb, width), 0)
    e = jnp.where((tok // NSA_BLOCK) == blk, 1.0, 0.0).astype(BF16)
    return _dot(chosen, e)


def _nsa_combine(g, o_c, o_s, o_w, tq):
    gate = jax.nn.sigmoid(g)
    lane = lax.broadcasted_iota(I32, (tq, LANE), 1)
    comb = []
    for h in range(B_HEADS):
        sl = slice(h * tq, (h + 1) * tq)
        comb.append(gate[:, 3 * h:3 * h + 1] * o_c[sl] + gate[:, 3 * h + 1:3 * h + 2] * o_s[sl]
                    + gate[:, 3 * h + 2:3 * h + 3] * o_w[sl])
    groups = [jnp.where(lane < B_HD, pltpu.roll(comb[2 * j], B_HD, 1), comb[2 * j + 1]) for j in range(2)]
    return jnp.concatenate(groups, axis=1)


def _nsa_prompt_kernel(qbp_ref, qbrp_ref, kcvc_ref, kvs_ref, kvw_ref, g_ref, o_ref, *, tq, tk):
    qi = pl.program_id(1)
    pos0 = qi * tq
    rows = B_HEADS * tq
    qc = jnp.concatenate([qbp_ref[:, h * LANE:(h + 1) * LANE] for h in range(B_HEADS)], axis=0)
    o_c, chosen = _nsa_compressed_branch(qc, kcvc_ref[...], pos0, tq, transposed_select=(tq % LANE == 0))
    qr = jnp.concatenate([qbrp_ref[:, h * LANE:(h + 1) * LANE] for h in range(B_HEADS)], axis=0)

    def masked(s, ok):
        w = s.shape[1]
        return jnp.where(ok[None], s.reshape(B_HEADS, tq, w), NEG_INF).reshape(rows, w)

    def values(kv):
        lane = lax.broadcasted_iota(I32, kv.shape, 1)
        return jnp.where(lane < B_HD, jnp.ones_like(kv), kv)

    qpos_s = pos0 + lax.broadcasted_iota(I32, (tq, tk), 0)
    c_s = lax.broadcasted_iota(I32, (tq, tk), 1)

    def sel_step(j, carry):
        m, acc = carry
        start = pl.multiple_of(j * tk, tk)
        kv = kvs_ref[pl.ds(start, tk), :]
        ok = (_expand_blocks(chosen, start, tk) > 0.5) & (start + c_s <= qpos_s)
        s = masked(_dot_nt(qr, kv), ok)
        m_new = jnp.maximum(m, jnp.max(s, axis=-1, keepdims=True))
        p = jnp.exp2(s - m_new)
        acc = jnp.exp2(m - m_new) * acc + _dot(p.astype(BF16), values(kv))
        return m_new, acc

    n_chunks = (pos0 + tq - 1) // tk + 1
    init = (jnp.full((rows, 1), NEG_INF, F32), jnp.zeros((rows, LANE), F32))
    _, acc = lax.fori_loop(0, n_chunks, sel_step, init)
    o_s = acc / acc[:, 0:1]

    span = WINDOW + tq
    start = pl.multiple_of(jnp.maximum(pos0 - WINDOW, 0), tq)
    kv = kvw_ref[pl.ds(start, span), :]
    qpos_w = pos0 + lax.broadcasted_iota(I32, (tq, span), 0)
    tok_w = start + lax.broadcasted_iota(I32, (tq, span), 1)
    s = masked(_dot_nt(qr, kv), (tok_w <= qpos_w) & (tok_w > qpos_w - WINDOW))
    p = jnp.exp2(s - jnp.max(s, axis=-1, keepdims=True))
    acc = _dot(p.astype(BF16), values(kv))
    o_w = acc / acc[:, 0:1]

    o_ref[...] = _nsa_combine(g_ref[...], o_c, o_s, o_w, tq).astype(BF16)


def _nsa_prompt(qbp, qbrp, kcvc, kvsb, kvwb, g, nbatch, seq, tq, tk):
    nq = seq // tq
    nb = seq // NSA_BLOCK
    assert tk % tq == 0 and seq % tk == 0 and seq >= WINDOW + tq and WINDOW % tq == 0
    tile = lambda w: pl.BlockSpec((tq, w), lambda b, i: (b * nq + i, 0))
    return pl.pallas_call(
        functools.partial(_nsa_prompt_kernel, tq=tq, tk=tk),
        grid=(nbatch, nq),
        in_specs=[tile(4 * LANE), tile(4 * LANE), pl.BlockSpec((None, nb, LANE), lambda b, i: (b, 0, 0)),
                  pl.BlockSpec((seq, LANE), lambda b, i: (b, 0)), pl.BlockSpec((seq, LANE), lambda b, i: (b, 0)),
                  tile(LANE)],
        out_specs=tile(B_W),
        out_shape=jax.ShapeDtypeStruct((nbatch * seq, B_W), BF16),
        compiler_params=_params("parallel", "arbitrary"),
        name="nsa_prompt",
    )(qbp, qbrp, kcvc, kvsb, kvwb, g)


def _nsa_sample_body(qbp_ref, qbrp_ref, g_ref, cn_ref, sn_ref, wn_ref, win_ref, pe_ref, w_ref, cmp_pages, slc_pages,
                     o_ref, msum_ref, *, n_new, past, nbp):
    n_pages = len(cmp_pages)
    tq = TOK_PAD
    rows = B_HEADS * tq
    page_rows = cmp_pages[0].shape[0]
    bpp = page_rows // NSA_BLOCK
    nb_past = n_pages * bpp

    for p, pg in enumerate(cmp_pages):
        msum_ref[p * bpp:(p + 1) * bpp, :] = jnp.sum(pg[...].reshape(bpp, NSA_BLOCK, LANE), axis=1)
    msum_ref[nb_past:nb_past + 1, :] = jnp.sum(cn_ref[...], axis=0, keepdims=True)
    msum_ref[nb_past + 1:nbp, :] = jnp.zeros((nbp - nb_past - 1, LANE), F32)
    mblk = msum_ref[...] * (1.0 / NSA_BLOCK) + jnp.mean(pe_ref[...], axis=0, keepdims=True)
    kcvc = jnp.dot(mblk, w_ref[...], preferred_element_type=F32, precision=lax.Precision.HIGHEST)

    qc = jnp.concatenate([qbp_ref[:, h * LANE:(h + 1) * LANE] for h in range(B_HEADS)], axis=0)
    o_c, chosen = _nsa_compressed_branch(qc, kcvc, past, tq)
    qr = jnp.concatenate([qbrp_ref[:, h * LANE:(h + 1) * LANE] for h in range(B_HEADS)], axis=0)

    def masked(s, ok):
        w = s.shape[1]
        return jnp.where(ok[None], s.reshape(B_HEADS, tq, w), NEG_INF).reshape(rows, w)

    tok_n = lax.broadcasted_iota(I32, (tq, NEW_ROWS), 0)
    j_n = lax.broadcasted_iota(I32, (tq, NEW_ROWS), 1)
    new_ok = (j_n <= tok_n) & (j_n < n_new)

    def attend(score_tiles, value_tiles):
        m = None
        for s in score_tiles:
            mx = jnp.max(s, axis=-1, keepdims=True)
            m = mx if m is None else jnp.maximum(m, mx)
        l = jnp.zeros((rows, 1), F32)
        acc = jnp.zeros((rows, LANE), F32)
        for s, v in zip(score_tiles, value_tiles):
            p = jnp.exp2(s - m)
            l = l + jnp.sum(p, axis=-1, keepdims=True)
            acc = acc + _dot(p.astype(BF16), v)
        return acc / l

    tiles, vals = [], []
    for p, pg in enumerate(slc_pages):
        kv = pg[...].astype(BF16)
        ok = _expand_blocks(chosen, p * page_rows, page_rows) > 0.5
        tiles.append(masked(_dot_nt(qr, kv), ok))
        vals.append(kv)
    kv = sn_ref[...].astype(BF16)
    ok = (_expand_blocks(chosen, past, NEW_ROWS) > 0.5) & new_ok
    tiles.append(masked(_dot_nt(qr, kv), ok))
    vals.append(kv)
    o_s = attend(tiles, vals)

    wb = win_ref.shape[0]
    kv = win_ref[...].astype(BF16)
    r_w = lax.broadcasted_iota(I32, (tq, wb), 1)
    t_w = lax.broadcasted_iota(I32, (tq, wb), 0)
    tiles = [masked(_dot_nt(qr, kv), r_w > t_w + (wb - WINDOW))]
    vals = [kv]
    kv = wn_ref[...].astype(BF16)
    tiles.append(masked(_dot_nt(qr, kv), new_ok))
    vals.append(kv)
    o_w = attend(tiles, vals)

    o_ref[...] = _nsa_combine(g_ref[...], o_c, o_s, o_w, tq).astype(BF16)


N_DIFF_SEQ_IN = 5
N_NSA_SEQ_IN = 9


def _sample_mixers_kernel(pt_ref, *refs, n_pages, n_new, past, nbp, lam_init):
    del pt_ref
    diff_in = refs[:N_DIFF_SEQ_IN]
    nsa_in = refs[N_DIFF_SEQ_IN:N_DIFF_SEQ_IN + N_NSA_SEQ_IN]
    pages = refs[N_DIFF_SEQ_IN + N_NSA_SEQ_IN:]
    k_pages, v_pages, cmp_pages, slc_pages = (pages[i * n_pages:(i + 1) * n_pages] for i in range(4))
    oa_ref, ob_ref, msum_ref = pages[4 * n_pages:]
    _diff_sample_body(*diff_in, k_pages, v_pages, oa_ref, n_new=n_new, lam_init=lam_init)
    _nsa_sample_body(*nsa_in, cmp_pages, slc_pages, ob_ref, msum_ref, n_new=n_new, past=past, nbp=nbp)


def _sample_mixers(pt_flat, lw, subw, q8, kn, vn, qbp8, qbrp8, g8, cn, sn, wn, state_win, pe_cat, w_bd, cache_k,
                   cache_v, cache_cmp, cache_slc, layer, lam_init, n_pages, n_new, past):
    nseq = q8.shape[0]
    new_rows = n_new * A_HEADS
    assert new_rows % 8 == 0
    wb = state_win.shape[2]
    nb = -(-(past + n_new) // NSA_BLOCK)
    nbp = -(-nb // 16) * 16

    def page_specs(cache):
        rows = cache.shape[2]
        return [pl.BlockSpec((None, None, rows, LANE), lambda b, pt, p=p: (layer, pt[b * n_pages + p], 0, 0))
                for p in range(n_pages)]

    seq_spec = lambda r, w: pl.BlockSpec((None, r, w), lambda b, pt: (b, 0, 0))
    const_spec = lambda r, w: pl.BlockSpec((r, w), lambda b, pt: (0, 0))
    caches = (cache_k, cache_v, cache_cmp, cache_slc)
    grid_spec = pltpu.PrefetchScalarGridSpec(
        num_scalar_prefetch=1,
        grid=(nseq,),
        in_specs=[const_spec(4, A_HD), const_spec(1, LANE), seq_spec(TOK_PAD, A_W), seq_spec(new_rows, LANE),
                  seq_spec(new_rows, LANE),
                  seq_spec(TOK_PAD, 4 * LANE), seq_spec(TOK_PAD, 4 * LANE), seq_spec(TOK_PAD, LANE),
                  seq_spec(NEW_ROWS, LANE), seq_spec(NEW_ROWS, LANE), seq_spec(NEW_ROWS, LANE),
                  pl.BlockSpec((None, None, wb, LANE), lambda b, pt: (layer, b, 0, 0)),
                  const_spec(NSA_BLOCK, LANE), const_spec(LANE, LANE)]
        + [s for c in caches for s in page_specs(c)],
        out_specs=[seq_spec(TOK_PAD, A_W), seq_spec(TOK_PAD, B_W)],
        scratch_shapes=[pltpu.VMEM((nbp, LANE), F32)],
    )
    return pl.pallas_call(
        functools.partial(_sample_mixers_kernel, n_pages=n_pages, n_new=n_new, past=past, nbp=nbp,
                          lam_init=lam_init),
        grid_spec=grid_spec,
        out_shape=[jax.ShapeDtypeStruct((nseq, TOK_PAD, A_W), BF16), jax.ShapeDtypeStruct((nseq, TOK_PAD, B_W), BF16)],
        compiler_params=_params("parallel"),
        name="sample_mixers",
    )(pt_flat, lw, subw, q8, kn, vn, qbp8, qbrp8, g8, cn, sn, wn, state_win, pe_cat, w_bd,
      *[c for c in caches for _ in range(n_pages)])


def _conv_prompt_kernel(c_ref, halo_ref, b_ref, w_ref, o_ref):
    i = pl.program_id(1)
    c = c_ref[...]
    rows = c.shape[0]
    halo = jnp.where(i > 0, halo_ref[...], 0.0)
    r = lax.broadcasted_iota(I32, c.shape, 0)
    c1 = jnp.where(r == 0, halo[7:8], pltpu.roll(c, 1, 0))
    c2 = jnp.where(r == 0, halo[6:7], jnp.where(r == 1, halo[7:8], pltpu.roll(c, 2, 0)))
    del rows
    z = w_ref[0:1] * c2 + w_ref[1:2] * c1 + w_ref[2:3] * c
    o_ref[...] = (b_ref[...] * z).astype(BF16)


def _conv_prompt(cin, bgate, conv_w, nbatch, seq, tm):
    nt = seq // tm
    hb = tm // 8
    tile = pl.BlockSpec((tm, C_W), lambda b, i: (b * nt + i, 0))
    return pl.pallas_call(
        _conv_prompt_kernel,
        grid=(nbatch, nt),
        in_specs=[tile, pl.BlockSpec((8, C_W), lambda b, i: (jnp.maximum((b * nt + i) * hb - 1, 0), 0)), tile,
                  pl.BlockSpec((CONV_W, C_W), lambda b, i: (0, 0))],
        out_specs=tile,
        out_shape=jax.ShapeDtypeStruct((nbatch * seq, C_W), BF16),
        compiler_params=_params("parallel", "parallel"),
        name="conv_prompt",
    )(cin, cin, bgate, conv_w)


def _conv_sample_kernel(c0_ref, c1_ref, c2_ref, b_ref, w_ref, o_ref):
    z = w_ref[0:1] * c0_ref[...] + w_ref[1:2] * c1_ref[...] + w_ref[2:3] * c2_ref[...]
    o_ref[...] = (b_ref[...] * z).astype(BF16)


def _conv_sample(c0, c1, c2, bgate, conv_w):
    return pl.pallas_call(
        _conv_sample_kernel,
        out_shape=jax.ShapeDtypeStruct(c0.shape, BF16),
        name="conv_sample",
    )(c0, c1, c2, bgate, conv_w)


def _mix_out_kernel(x_ref, oa_ref, ob_ref, oc_ref, wout_ref, nw_ref, wq_ref, x_out_ref, q_out_ref):
    y = (_dot(oa_ref[...], wout_ref[0:A_W]) + _dot(ob_ref[...], wout_ref[A_W:A_W + B_W])
         + _dot(oc_ref[...], wout_ref[A_W + B_W:A_W + B_W + C_W]))
    x = x_ref[...] + _rms(y, nw_ref[NORM_MIX_POST:NORM_MIX_POST + 1])
    x_out_ref[...] = x
    h = _rms(x, nw_ref[NORM_X_PRE:NORM_X_PRE + 1]).astype(BF16)
    q_out_ref[...] = (_dot(h, wq_ref[...]) * Q_SCALE).astype(BF16)


def _mix_out(x, mixed, w_out, nw8, w_q, tm):
    n, d = x.shape
    row = lambda w: pl.BlockSpec((tm, w), lambda i: (i, 0))
    full = lambda a: pl.BlockSpec(a.shape, lambda i: (0, 0))
    return pl.pallas_call(
        _mix_out_kernel,
        grid=(n // tm,),
        in_specs=[row(d)] + [row(a.shape[1]) for a in mixed] + [full(w_out), full(nw8), full(w_q)],
        out_specs=[row(d), row(X_W)],
        out_shape=[jax.ShapeDtypeStruct((n, d), F32), jax.ShapeDtypeStruct((n, X_W), BF16)],
        compiler_params=_params("parallel"),
        name="mix_out",
    )(x, *mixed, w_out, nw8, w_q)


def _cross_attn_kernel(q_ref, mkt_ref, mvt_ref, o_ref):
    q = q_ref[...]
    ts = q.shape[0]
    lane = lax.broadcasted_iota(I32, q.shape, 1)
    zero = jnp.zeros_like(q)
    qs = jnp.concatenate(
        [jnp.where((lane >= h * X_HD) & (lane < (h + 1) * X_HD), q, zero) for h in range(X_HEADS)], axis=0)
    s = _dot(qs, mkt_ref[...].astype(BF16))
    e = jnp.exp2(s - jnp.max(s, axis=-1, keepdims=True))
    p = e / jnp.sum(e, axis=-1, keepdims=True)
    o = _dot_nt(p.astype(BF16), mvt_ref[...].astype(BF16))
    out = jnp.zeros((ts, X_W), F32)
    for h in range(X_HEADS):
        out = out + jnp.where((lane >= h * X_HD) & (lane < (h + 1) * X_HD), o[h * ts:(h + 1) * ts], 0.0)
    o_ref[...] = out.astype(BF16)


def _cross_attn(q, mkt, mvt, ts, v_block=0, b_off=0):
    nbatch, seq, _ = q.shape
    n_mem = mkt.shape[2]
    return pl.pallas_call(
        _cross_attn_kernel,
        grid=(nbatch, seq // ts),
        in_specs=[pl.BlockSpec((None, ts, X_W), lambda b, i: (b, i, 0)),
                  pl.BlockSpec((None, X_W, n_mem), lambda b, i: (b_off + b, 0, 0)),
                  pl.BlockSpec((None, X_W, n_mem), lambda b, i: (b_off + b, v_block, 0))],
        out_specs=pl.BlockSpec((None, ts, X_W), lambda b, i: (b, i, 0)),
        out_shape=jax.ShapeDtypeStruct((nbatch, seq, X_W), BF16),
        compiler_params=_params("parallel", "parallel"),
        name="cross_attn",
    )(q, mkt, mvt)


def _mem_kv_kernel(m_ref, nw_ref, wt_ref, o_ref):
    h = _rms(m_ref[...], nw_ref[...]).astype(BF16)
    o_ref[...] = _dot_nt(wt_ref[...], h)


def _mem_kv(mem, nw, w_kv_t):
    nbatch, n_mem, d = mem.shape
    return pl.pallas_call(
        _mem_kv_kernel,
        grid=(nbatch,),
        in_specs=[pl.BlockSpec((None, n_mem, d), lambda b: (b, 0, 0)), pl.BlockSpec((1, d), lambda b: (0, 0)),
                  pl.BlockSpec(w_kv_t.shape, lambda b: (0, 0))],
        out_specs=pl.BlockSpec((None, 2 * X_W, n_mem), lambda b: (b, 0, 0)),
        out_shape=jax.ShapeDtypeStruct((nbatch, 2 * X_W, n_mem), F32),
        compiler_params=_params("parallel"),
        name="mem_kv",
    )(mem, nw, w_kv_t)


def _ffn_kernel(x_ref, o_ref, wo_ref, nw_ref, wg_ref, wu_ref, wd_ref, y_ref, x2_ref, h_ref, acc_ref):
    j = pl.program_id(1)

    @pl.when(j == 0)
    def _():
        x2 = x_ref[...] + _rms(_dot(o_ref[...], wo_ref[...]), nw_ref[NORM_X_POST:NORM_X_POST + 1])
        x2_ref[...] = x2
        h_ref[...] = _rms(x2, nw_ref[NORM_FFN_PRE:NORM_FFN_PRE + 1]).astype(BF16)
        acc_ref[...] = jnp.zeros_like(acc_ref)

    h = h_ref[...]
    g = _dot(h, wg_ref[...])
    u = _dot(h, wu_ref[...])
    a = (g * jax.nn.sigmoid(g)) * u
    acc_ref[...] += _dot(a.astype(BF16), wd_ref[...])

    @pl.when(j == pl.num_programs(1) - 1)
    def _():
        y_ref[...] = x2_ref[...] + _rms(acc_ref[...], nw_ref[NORM_FFN_POST:NORM_FFN_POST + 1])


def _ffn(x, o, w_o, nw8, w_g, w_u, w_d, tm, tf):
    n, d = x.shape
    dff = w_g.shape[1]
    row = lambda w: pl.BlockSpec((tm, w), lambda i, j: (i, 0))
    return pl.pallas_call(
        _ffn_kernel,
        grid=(n // tm, dff // tf),
        in_specs=[row(d), row(X_W), pl.BlockSpec(w_o.shape, lambda i, j: (0, 0)),
                  pl.BlockSpec(nw8.shape, lambda i, j: (0, 0)),
                  pl.BlockSpec((d, tf), lambda i, j: (0, j)), pl.BlockSpec((d, tf), lambda i, j: (0, j)),
                  pl.BlockSpec((tf, d), lambda i, j: (j, 0))],
        out_specs=row(d),
        out_shape=jax.ShapeDtypeStruct((n, d), F32),
        scratch_shapes=[pltpu.VMEM((tm, d), F32), pltpu.VMEM((tm, d), BF16), pltpu.VMEM((tm, d), F32)],
        compiler_params=_params("parallel", "arbitrary"),
        name="ffn",
    )(x, o, w_o, nw8, w_g, w_u, w_d)


def _rope_tables(pos):
    inv = ROPE_THETA ** (-jnp.arange(ROPE_HALF, dtype=F32) / ROPE_HALF)
    ang = pos.astype(F32)[:, None] * inv[None, :]
    cos, sin = jnp.cos(ang), jnp.sin(ang)
    n = pos.shape[0]
    rest = A_HD - 2 * ROPE_HALF
    z8 = jnp.zeros((n, ROPE_HALF), F32)
    c = jnp.concatenate([cos, cos, jnp.ones((n, rest), F32)], axis=1)
    s1 = jnp.concatenate([-sin, z8, jnp.zeros((n, rest), F32)], axis=1)
    s2 = jnp.concatenate([z8, sin, jnp.zeros((n, rest), F32)], axis=1)
    return jnp.concatenate([c, c, s1, s1, s2, s2], axis=1)


def _pad_rows(t, rows):
    return jnp.pad(t, ((0, 0), (0, rows - t.shape[1]), (0, 0)))


def kernel(x_prompt, x_sample, cache_diff_k, cache_diff_v, cache_nsa_cmp, cache_nsa_slc, state_nsa_win, state_conv,
           cache_mem_k, cache_mem_v, page_table, mem_prompt, norm_w, w_in, w_out, diff_lambda, diff_subln, nsa_pe,
           nsa_w_cmp, conv_w, w_q_mem, w_kv_mem, w_o_mem, w_ffn_up, w_ffn_down):
    depth = w_in.shape[0]
    nbp, seq, d = x_prompt.shape
    nbs, sseq, _ = x_sample.shape
    n_p, n_s = nbp * seq, nbs * sseq
    n_pool, page_rows = cache_diff_k.shape[1], cache_diff_k.shape[2]
    n_pages = page_table.shape[1]
    past = n_pages * page_rows
    n_mem = mem_prompt.shape[1]
    dff = w_ffn_down.shape[1]
    assert seq % 256 == 0 and seq >= WINDOW and sseq <= TOK_PAD and d == 1024

    tm_p = _pick_tile(seq, 512)
    tm_s = _pick_tile(n_s, 512)
    tf = _pick_tile(dff, 1408, 128)

    split = 2176
    w_in_p = jnp.concatenate(
        [w_in[:, :, :split], w_in[:, :, split + 3 * B_HEADS:], w_in[:, :, split:split + 3 * B_HEADS],
         jnp.zeros((depth, d, IN_COLS_PADDED - w_in.shape[2]), w_in.dtype)], axis=2).astype(BF16)
    w_out_b = w_out.astype(BF16)
    w_q_b = w_q_mem.astype(BF16)
    w_kv_t = jnp.swapaxes(w_kv_mem, 1, 2).astype(BF16)
    mkt_s, mvt_s = (jnp.transpose(t, (0, 1, 3, 4, 2)).reshape(depth * nbs, X_W, n_mem)
                    for t in (cache_mem_k, cache_mem_v))
    w_o_b = w_o_mem.astype(BF16)
    w_g_b = w_ffn_up[:, :, :dff].astype(BF16)
    w_u_b = w_ffn_up[:, :, dff:].astype(BF16)
    w_d_b = w_ffn_down.astype(BF16)
    nw8 = jnp.pad(norm_w, ((0, 0), (0, 1), (0, 0)))
    pe_cat = jnp.concatenate([nsa_pe[:, 0], nsa_pe[:, 1]], axis=-1)
    zb = jnp.zeros((depth, B_HD, B_HD), F32)
    w_bd = jnp.concatenate([jnp.concatenate([nsa_w_cmp[:, 0], zb], axis=2),
                            jnp.concatenate([zb, nsa_w_cmp[:, 1]], axis=2)], axis=1)
    subw = diff_subln.reshape(depth, 1, 2 * A_HD)

    rope_p = _rope_tables(jnp.arange(seq, dtype=I32))
    rope_s = _rope_tables(jnp.tile(past + jnp.arange(sseq, dtype=I32), nbs))
    pt_flat = page_table.reshape(-1).astype(I32)
    ck = cache_diff_k.reshape(depth, n_pool, page_rows * A_HEADS, 2 * A_HD)
    cv = cache_diff_v.reshape(depth, n_pool, page_rows * A_HEADS, 2 * A_HD)

    xp = x_prompt.reshape(n_p, d)
    xs = x_sample.reshape(n_s, d)
    outs = [[] for _ in range(14)]

    def sample3(t, rows):
        return _pad_rows(t.reshape(nbs, sseq, t.shape[1]), rows)

    for l in range(depth):
        lam_init = 0.8 - 0.6 * math.exp(-0.3 * l)
        nw_pre = norm_w[l, NORM_MIX_PRE:NORM_MIX_PRE + 1]

        (qa, ka, kab, va, vab, qbp, qbrp, kvc, kvs, kvsb, kvw, kvwb, g, cin, bgate) = _proj_in(
            xp, nw_pre, w_in_p[l], rope_p, tm_p)
        oa_p = _diff_prompt(qa, kab, vab, diff_lambda[l], subw[l], lam_init, nbp, seq, 512, 512)
        kcvc = _nsa_compress(kvc, pe_cat[l], w_bd[l], nbp, seq)
        ob_p = _nsa_prompt(qbp, qbrp, kcvc, kvsb, kvwb, g, nbp, seq, 256, 512)
        oc_p = _conv_prompt(cin, bgate, conv_w[l], nbp, seq, _pick_tile(seq, 512))
        xp, qx_p = _mix_out(xp, (oa_p, ob_p, oc_p), w_out_b[l], nw8[l], w_q_b[l], tm_p)
        mkv_t = _mem_kv(mem_prompt, norm_w[l, NORM_MEM:NORM_MEM + 1], w_kv_t[l])
        ox_p = _cross_attn(qx_p.reshape(nbp, seq, X_W), mkv_t, mkv_t, _pick_tile(seq, 512), v_block=1)
        xp = _ffn(xp, ox_p.reshape(n_p, X_W), w_o_b[l], nw8[l], w_g_b[l], w_u_b[l], w_d_b[l], tm_p, tf)
        mk_p, mv_p = (jnp.transpose(t.reshape(nbp, X_HEADS, X_HD, n_mem), (0, 3, 1, 2))
                      for t in (mkv_t[:, :X_W], mkv_t[:, X_W:]))

        (qa_s, ka_s, _, va_s, _, qbp_s, qbrp_s, kvc_s, kvs_s, _, kvw_s, _, g_s, cin_s, bgate_s) = _proj_in(
            xs, nw_pre, w_in_p[l], rope_s, tm_s)
        new_heads = lambda t: t.reshape(nbs, sseq * A_HEADS, 2 * A_HD)
        oa_s, ob_s = _sample_mixers(
            pt_flat, diff_lambda[l], subw[l], sample3(qa_s, TOK_PAD), new_heads(ka_s), new_heads(va_s),
            sample3(qbp_s, TOK_PAD), sample3(qbrp_s, TOK_PAD), sample3(g_s, TOK_PAD), sample3(kvc_s, NEW_ROWS),
            sample3(kvs_s, NEW_ROWS), sample3(kvw_s, NEW_ROWS), state_nsa_win, pe_cat[l], w_bd[l], ck, cv,
            cache_nsa_cmp, cache_nsa_slc, l, lam_init, n_pages, sseq, past)
        c_all = jnp.concatenate([state_conv[l], cin_s.reshape(nbs, sseq, C_W)], axis=1)
        oc_s = _conv_sample(*[c_all[:, j:j + sseq].reshape(n_s, C_W) for j in range(CONV_W)], bgate_s, conv_w[l])
        xs, qx_s = _mix_out(xs, (oa_s[:, :sseq].reshape(n_s, A_W), ob_s[:, :sseq].reshape(n_s, B_W), oc_s),
                            w_out_b[l], nw8[l], w_q_b[l], tm_s)
        ox_s = _cross_attn(_pad_rows(qx_s.reshape(nbs, sseq, X_W), TOK_PAD), mkt_s, mvt_s, TOK_PAD, b_off=l * nbs)
        xs = _ffn(xs, ox_s[:, :sseq].reshape(n_s, X_W), w_o_b[l], nw8[l], w_g_b[l], w_u_b[l], w_d_b[l], tm_s, tf)

        win_p = kvw.reshape(nbp, seq, 2 * B_HD)[:, seq - WINDOW:]
        cin_p = cin.reshape(nbp, seq, C_W)
        win_s = kvw_s.reshape(nbs, sseq, 2 * B_HD)
        layer_out = (
            ka.reshape(nbp, seq, A_HEADS, 2 * A_HD), va.reshape(nbp, seq, A_HEADS, 2 * A_HD),
            kvc.reshape(nbp, seq, 2 * B_HD), kvs.reshape(nbp, seq, 2 * B_HD), win_p,
            cin_p[:, seq - (CONV_W - 1):], mk_p, mv_p,
            ka_s.reshape(nbs, sseq, A_HEADS, 2 * A_HD), va_s.reshape(nbs, sseq, A_HEADS, 2 * A_HD),
            kvc_s.reshape(nbs, sseq, 2 * B_HD), kvs_s.reshape(nbs, sseq, 2 * B_HD), win_s, c_all[:, sseq:])
        for acc, t in zip(outs, layer_out):
            acc.append(t)

    stacked = [jnp.stack(t) for t in outs]
    stacked[12] = jnp.concatenate([state_nsa_win, stacked[12]], axis=2)[:, :, sseq:]
    return (xp.reshape(nbp, seq, d), xs.reshape(nbs, sseq, d), *stacked)
```

```python
import functools
import math

import jax
import jax.numpy as jnp
from jax import lax
from jax.experimental import pallas as pl
from jax.experimental.pallas import tpu as pltpu

F32 = jnp.float32
BF16 = jnp.bfloat16
I32 = jnp.int32

A_HEADS = 4
A_HD = 64
A_W = A_HEADS * 2 * A_HD
B_HEADS = 4
B_HD = 64
B_W = B_HEADS * B_HD
NSA_BLOCK = 64
NSA_TOPK = 16
WINDOW = 512
C_W = 256
CONV_W = 3
X_HEADS = 4
X_HD = 64
X_W = X_HEADS * X_HD
ROPE_THETA = 500000.0
ROPE_HALF = (A_HD // 4) // 2
EPS = 1e-6
NEG_INF = -1e30
FORCE_SCORE = 1e4
Q_SCALE = A_HD ** -0.5 * math.log2(math.e)
LANE = 128
IN_COLS_PADDED = 3072
NEW_ROWS = 16
TOK_PAD = 8
VMEM_LIMIT = 56 * 1024 * 1024

NORM_MIX_PRE, NORM_MIX_POST, NORM_X_PRE, NORM_X_POST, NORM_FFN_PRE, NORM_FFN_POST, NORM_MEM = range(7)

NT_DIMS = (((1,), (1,)), ((), ()))


def _params(*sem):
    return pltpu.CompilerParams(dimension_semantics=sem, vmem_limit_bytes=VMEM_LIMIT)


def _pick_tile(n, target, mult=16):
    best = None
    for t in range(mult, min(n, target) + 1, mult):
        if n % t == 0:
            best = t
    assert best is not None, (n, target)
    return best


def _rms(x, w):
    return x * lax.rsqrt(jnp.mean(x * x, axis=-1, keepdims=True) + EPS) * w


def _dot(a, b):
    return jnp.dot(a, b, preferred_element_type=F32)


def _dot_nt(a, b):
    return lax.dot_general(a, b, NT_DIMS, preferred_element_type=F32)


def _rope(z, c, s1, s2):
    outs = []
    for j in range(z.shape[1] // LANE):
        t = z[:, j * LANE:(j + 1) * LANE]
        outs.append(t * c + pltpu.roll(t, LANE - ROPE_HALF, 1) * s1 + pltpu.roll(t, ROPE_HALF, 1) * s2)
    return outs[0] if len(outs) == 1 else jnp.concatenate(outs, axis=1)


def _softmax_update(s, v, m, l, acc):
    m_new = jnp.maximum(m, jnp.max(s, axis=-1, keepdims=True))
    p = jnp.exp2(s - m_new)
    alpha = jnp.exp2(m - m_new)
    l = alpha * l + jnp.sum(p, axis=-1, keepdims=True)
    acc = alpha * acc + _dot(p.astype(BF16), v)
    return m_new, l, acc


def _proj_in_kernel(x_ref, nw_ref, w_ref, rope_ref, qa_ref, ka_ref, kab_ref, va_ref, vab_ref, qbp_ref, qbrp_ref,
                    kvc_ref, kvs_ref, kvsb_ref, kvw_ref, kvwb_ref, g_ref, cin_ref, bgate_ref):
    h = _rms(x_ref[...], nw_ref[...]).astype(BF16)
    c = rope_ref[:, 0:LANE]
    s1 = rope_ref[:, LANE:2 * LANE]
    s2 = rope_ref[:, 2 * LANE:3 * LANE]

    def mm(a, b):
        return _dot(h, w_ref[:, a:b])

    qa_ref[...] = (_rope(mm(0, 512), c, s1, s2) * Q_SCALE).astype(BF16)
    zk = _rope(mm(512, 1024), c, s1, s2)
    def store_head_rows(ref, z):
        for hd in range(A_HEADS):
            ref[pl.ds(hd, z.shape[0], stride=A_HEADS), :] = z[:, hd * LANE:(hd + 1) * LANE]

    store_head_rows(ka_ref, zk)
    kab_ref[...] = zk.astype(BF16)
    zv = mm(1024, 1536)
    store_head_rows(va_ref, zv)
    vab_ref[...] = zv.astype(BF16)

    zb = mm(1536, 1792) * Q_SCALE
    zbr = _rope(zb, c, s1, s2)
    first = lax.broadcasted_iota(I32, (zb.shape[0], LANE), 1) < B_HD

    def pad_heads(z):
        outs = []
        for j in range(2):
            t = z[:, j * LANE:(j + 1) * LANE]
            outs.append(jnp.where(first, t, 0.0))
            outs.append(jnp.where(first, pltpu.roll(t, B_HD, 1), 0.0))
        return jnp.concatenate(outs, axis=1)

    qbp_ref[...] = pad_heads(zb).astype(BF16)
    qbrp_ref[...] = pad_heads(zbr).astype(BF16)

    zkv = mm(1792, 2176)
    kvc_ref[...] = zkv[:, 0:LANE]
    ckv = jnp.where(first, c, 1.0)
    s1kv = jnp.where(first, s1, 0.0)
    s2kv = jnp.where(first, s2, 0.0)
    ks = _rope(zkv[:, LANE:2 * LANE], ckv, s1kv, s2kv)
    kvs_ref[...] = ks
    kvsb_ref[...] = ks.astype(BF16)
    kw = _rope(zkv[:, 2 * LANE:3 * LANE], ckv, s1kv, s2kv)
    kvw_ref[...] = kw
    kvwb_ref[...] = kw.astype(BF16)

    zc = mm(2176, 2944)
    cin_ref[...] = zc[:, 2 * C_W:3 * C_W] * zc[:, 0:C_W]
    bgate_ref[...] = zc[:, C_W:2 * C_W]
    g_ref[...] = mm(2944, 3072)


def _proj_in(x, nw, w, rope, tm):
    n, d = x.shape
    period_tiles = rope.shape[0] // tm
    assert rope.shape[0] % tm == 0
    row = lambda wdt: pl.BlockSpec((tm, wdt), lambda i: (i, 0))
    rope_spec = pl.BlockSpec((tm, 3 * LANE), lambda i: (i % period_tiles, 0))
    head_rows = pl.BlockSpec((tm * A_HEADS, LANE), lambda i: (i, 0))
    widths = [(512, BF16), (512, F32), (512, BF16), (512, F32), (512, BF16), (512, BF16), (512, BF16),
              (128, F32), (128, F32), (128, BF16), (128, F32), (128, BF16), (128, F32), (C_W, F32), (C_W, F32)]
    return pl.pallas_call(
        _proj_in_kernel,
        grid=(n // tm,),
        in_specs=[row(d), pl.BlockSpec((1, d), lambda i: (0, 0)),
                  pl.BlockSpec((d, IN_COLS_PADDED), lambda i: (0, 0)), rope_spec],
        out_specs=[head_rows if j in (1, 3) else row(wd) for j, (wd, _) in enumerate(widths)],
        out_shape=[jax.ShapeDtypeStruct((n * A_HEADS, LANE) if j in (1, 3) else (n, wd), dt)
                   for j, (wd, dt) in enumerate(widths)],
        compiler_params=_params("parallel"),
        name="proj_in",
    )(x, nw, w, rope)


def _diff_lambda(lw):
    a = jnp.sum(lw[0:1] * lw[1:2], axis=-1, keepdims=True)
    b = jnp.sum(lw[2:3] * lw[3:4], axis=-1, keepdims=True)
    return jnp.exp(a) - jnp.exp(b)


def _diff_finish(o1, o2, lam, subw, one_minus_init):
    d = o1 - lam * o2
    return _rms(d, subw) * one_minus_init


def _diff_prompt_kernel(lw_ref, q_ref, k_ref, v_ref, subw_ref, o_ref, *, tq, tk, lam_init):
    qi = pl.program_id(2)
    q = q_ref[...]
    lane = lax.broadcasted_iota(I32, (tq, LANE), 1)
    zero = jnp.zeros_like(q)
    qs = jnp.concatenate([jnp.where(lane < A_HD, q, zero), jnp.where(lane >= A_HD, q, zero)], axis=0)
    rows = 2 * tq

    def step(j, carry, diag):
        m, l, acc = carry
        start = pl.multiple_of(j * tk, tk)
        s = _dot_nt(qs, k_ref[pl.ds(start, tk), :])
        if diag:
            r = lax.broadcasted_iota(I32, (rows, tk), 0) & (tq - 1)
            cidx = lax.broadcasted_iota(I32, (rows, tk), 1)
            s = jnp.where(start + cidx <= qi * tq + r, s, NEG_INF)
        return _softmax_update(s, v_ref[pl.ds(start, tk), :], m, l, acc)

    n_full = (qi * tq) // tk
    init = (jnp.full((rows, 1), NEG_INF, F32), jnp.zeros((rows, 1), F32), jnp.zeros((rows, LANE), F32))
    carry = lax.fori_loop(0, n_full, lambda j, cr: step(j, cr, False), init)
    m, l, acc = step(n_full, carry, True)
    o = acc / l
    lam = _diff_lambda(lw_ref[...]) + lam_init
    o_ref[...] = _diff_finish(o[:tq], o[tq:], lam, subw_ref[...], 1.0 - lam_init).astype(BF16)


def _diff_prompt(qa, kab, vab, lw, subw, lam_init, nbatch, seq, tq, tk):
    nq = seq // tq
    assert tk % tq == 0 and seq % tk == 0
    return pl.pallas_call(
        functools.partial(_diff_prompt_kernel, tq=tq, tk=tk, lam_init=lam_init),
        grid=(nbatch, A_HEADS, nq),
        in_specs=[pl.BlockSpec((4, A_HD), lambda b, h, i: (0, 0)),
                  pl.BlockSpec((tq, LANE), lambda b, h, i: (b * nq + i, h)),
                  pl.BlockSpec((seq, LANE), lambda b, h, i: (b, h)),
                  pl.BlockSpec((seq, LANE), lambda b, h, i: (b, h)),
                  pl.BlockSpec((1, LANE), lambda b, h, i: (0, 0))],
        out_specs=pl.BlockSpec((tq, LANE), lambda b, h, i: (b * nq + i, h)),
        out_shape=jax.ShapeDtypeStruct((nbatch * seq, A_W), BF16),
        compiler_params=_params("parallel", "parallel", "arbitrary"),
        name="diff_prompt",
    )(lw, qa, kab, vab, subw)


def _diff_sample_body(lw_ref, subw_ref, q_ref, kn_ref, vn_ref, k_pages, v_pages, o_ref, *, n_new, lam_init):
    q = q_ref[...]
    lane = lax.broadcasted_iota(I32, (TOK_PAD, LANE), 1)
    pieces = []
    for h in range(A_HEADS):
        qh = q[:, h * LANE:(h + 1) * LANE]
        pieces += [jnp.where(lane < A_HD, qh, jnp.zeros_like(qh)), jnp.where(lane >= A_HD, qh, jnp.zeros_like(qh))]
    qs = jnp.concatenate(pieces, axis=0)
    rows = qs.shape[0]
    rows_per_head = 2 * TOK_PAD

    def own_head(width):
        r = lax.broadcasted_iota(I32, (rows, width), 0)
        c = lax.broadcasted_iota(I32, (rows, width), 1)
        return (c & (A_HEADS - 1)) == (r // rows_per_head), r, c

    ok_page, _, _ = own_head(k_pages[0].shape[0])
    s_past = [jnp.where(ok_page, _dot_nt(qs, kp[...].astype(BF16)), NEG_INF) for kp in k_pages]
    ok_new, r_new, c_new = own_head(n_new * A_HEADS)
    ok_new = ok_new & ((c_new // A_HEADS) <= (r_new & (TOK_PAD - 1)))
    s_new = jnp.where(ok_new, _dot_nt(qs, kn_ref[...].astype(BF16)), NEG_INF)
    m = jnp.max(s_new, axis=-1, keepdims=True)
    for s in s_past:
        m = jnp.maximum(m, jnp.max(s, axis=-1, keepdims=True))
    p_new = jnp.exp2(s_new - m)
    l = jnp.sum(p_new, axis=-1, keepdims=True)
    acc = _dot(p_new.astype(BF16), vn_ref[...].astype(BF16))
    for s, vp in zip(s_past, v_pages):
        p = jnp.exp2(s - m)
        l = l + jnp.sum(p, axis=-1, keepdims=True)
        acc = acc + _dot(p.astype(BF16), vp[...].astype(BF16))
    o = acc / l
    lam = _diff_lambda(lw_ref[...]) + lam_init
    outs = []
    for h in range(A_HEADS):
        o1 = o[(2 * h) * TOK_PAD:(2 * h + 1) * TOK_PAD]
        o2 = o[(2 * h + 1) * TOK_PAD:(2 * h + 2) * TOK_PAD]
        outs.append(_diff_finish(o1, o2, lam, subw_ref[...], 1.0 - lam_init))
    o_ref[...] = jnp.concatenate(outs, axis=1).astype(BF16)


def _nsa_compress_kernel(kv_ref, pe_ref, w_ref, o_ref, *, nb):
    kv = kv_ref[...]
    m = jnp.sum(kv.reshape(nb, NSA_BLOCK, LANE), axis=1) * (1.0 / NSA_BLOCK)
    m = m + jnp.mean(pe_ref[...], axis=0, keepdims=True)
    o_ref[...] = jnp.dot(m, w_ref[...], preferred_element_type=F32, precision=lax.Precision.HIGHEST)


def _nsa_compress(kvc, pe_cat, w_bd, nbatch, seq):
    nb = seq // NSA_BLOCK
    return pl.pallas_call(
        functools.partial(_nsa_compress_kernel, nb=nb),
        grid=(nbatch,),
        in_specs=[pl.BlockSpec((seq, LANE), lambda b: (b, 0)), pl.BlockSpec((NSA_BLOCK, LANE), lambda b: (0, 0)),
                  pl.BlockSpec((LANE, LANE), lambda b: (0, 0))],
        out_specs=pl.BlockSpec((None, nb, LANE), lambda b: (b, 0, 0)),
        out_shape=jax.ShapeDtypeStruct((nbatch, nb, LANE), F32),
        compiler_params=_params("parallel"),
        name="nsa_compress",
    )(kvc, pe_cat, w_bd)


def _select_blocks_transposed(qc, kc, pos0, tq):
    nb = kc.shape[0]
    rows = B_HEADS * tq
    sc = _dot_nt(kc, qc)
    qpos = pos0 + (lax.broadcasted_iota(I32, (nb, rows), 1) & (tq - 1))
    blk = lax.broadcasted_iota(I32, (nb, rows), 0)
    scm = jnp.where((blk + 1) * NSA_BLOCK <= qpos + 1, sc, NEG_INF)
    e = jnp.exp2(scm - jnp.max(scm, axis=0, keepdims=True))
    pc = e / jnp.sum(e, axis=0, keepdims=True)
    pc = pc * jnp.where(qpos + 1 >= NSA_BLOCK, 1.0, 0.0)
    imp = pc[:, 0:tq]
    for h in range(1, B_HEADS):
        imp = imp + pc[:, h * tq:(h + 1) * tq]
    qpos_t = pos0 + lax.broadcasted_iota(I32, (nb, tq), 1)
    blk_t = lax.broadcasted_iota(I32, (nb, tq), 0)
    cur = (qpos_t // NSA_BLOCK) == blk_t
    valid_t = (blk_t + 1) * NSA_BLOCK <= qpos_t + 1
    sel = jnp.where(cur, FORCE_SCORE, jnp.where(valid_t, imp, NEG_INF))
    blk_col = lax.broadcasted_iota(I32, (nb, 1), 0)
    rank = jnp.zeros((nb, tq), I32)
    for i in range(nb):
        row = sel[i:i + 1, :]
        rank = rank + jnp.where(row > sel, 1, jnp.where(row == sel, jnp.where(blk_col > i, 1, 0), 0))
    chosen_t = jnp.where((rank < NSA_TOPK) & (sel > NEG_INF / 2), 1.0, 0.0).astype(BF16)
    eye = jnp.where(lax.broadcasted_iota(I32, (tq, tq), 0) == lax.broadcasted_iota(I32, (tq, tq), 1), 1.0, 0.0)
    return _dot_nt(eye.astype(BF16), chosen_t).astype(BF16)


def _nsa_compressed_branch(qc, kcvc, pos0, tq, transposed_select=False):
    nb = kcvc.shape[0]
    rows = B_HEADS * tq
    lane = lax.broadcasted_iota(I32, kcvc.shape, 1)
    kc = jnp.where(lane < B_HD, kcvc, 0.0).astype(BF16)
    vc = jnp.where(lane >= B_HD, kcvc, 0.0).astype(BF16)
    sc = _dot_nt(qc, kc)
    qpos = pos0 + (lax.broadcasted_iota(I32, (rows, nb), 0) & (tq - 1))
    blk = lax.broadcasted_iota(I32, (rows, nb), 1)
    valid = (blk + 1) * NSA_BLOCK <= qpos + 1
    scm = jnp.where(valid, sc, NEG_INF)
    e = jnp.exp2(scm - jnp.max(scm, axis=-1, keepdims=True))
    pc = e / jnp.sum(e, axis=-1, keepdims=True)
    pc = pc * jnp.where(qpos + 1 >= NSA_BLOCK, 1.0, 0.0)
    o_c = _dot(pc.astype(BF16), vc)
    if transposed_select:
        return o_c, _select_blocks_transposed(qc, kc, pos0, tq)
    imp = jnp.sum(pc.reshape(B_HEADS, tq, nb), axis=0)
    qpos_t = pos0 + lax.broadcasted_iota(I32, (tq, nb), 0)
    blk_t = lax.broadcasted_iota(I32, (tq, nb), 1)
    cur = (qpos_t // NSA_BLOCK) == blk_t
    valid_t = (blk_t + 1) * NSA_BLOCK <= qpos_t + 1
    sel = jnp.where(cur, FORCE_SCORE, jnp.where(valid_t, imp, NEG_INF))
    blk_row = lax.broadcasted_iota(I32, (1, nb), 1)
    rank = jnp.zeros((tq, nb), I32)
    for i in range(nb):
        col = sel[:, i:i + 1]
        rank = rank + jnp.where(col > sel, 1, jnp.where(col == sel, jnp.where(blk_row > i, 1, 0), 0))
    chosen = (rank < NSA_TOPK) & (sel > NEG_INF / 2)
    return o_c, jnp.where(chosen, 1.0, 0.0).astype(BF16)


def _expand_blocks(chosen, first_tok, width):
    nb = chosen.shape[1]
    tok = first_tok + lax.broadcasted_iota(I32, (nb, width), 1)
    blk = lax.broadcasted_iota(I32, (nb, width), 0)
    e = jnp.where((tok // NSA_BLOCK) == blk, 1.0, 0.0).astype(BF16)
    return _dot(chosen, e)


def _nsa_combine(g, o_c, o_s, o_w, tq):
    gate = jax.nn.sigmoid(g)
    lane = lax.broadcasted_iota(I32, (tq, LANE), 1)
    comb = []
    for h in range(B_HEADS):
        sl = slice(h * tq, (h + 1) * tq)
        comb.append(gate[:, 3 * h:3 * h + 1] * o_c[sl] + gate[:, 3 * h + 1:3 * h + 2] * o_s[sl]
                    + gate[:, 3 * h + 2:3 * h + 3] * o_w[sl])
    groups = [jnp.where(lane < B_HD, pltpu.roll(comb[2 * j], B_HD, 1), comb[2 * j + 1]) for j in range(2)]
    return jnp.concatenate(groups, axis=1)


def _nsa_prompt_kernel(qbp_ref, qbrp_ref, kcvc_ref, kvs_ref, kvw_ref, g_ref, o_ref, *, tq, tk):
    qi = pl.program_id(1)
    pos0 = qi * tq
    rows = B_HEADS * tq
    qc = jnp.concatenate([qbp_ref[:, h * LANE:(h + 1) * LANE] for h in range(B_HEADS)], axis=0)
    o_c, chosen = _nsa_compressed_branch(qc, kcvc_ref[...], pos0, tq, transposed_select=(tq % LANE == 0))
    qr = jnp.concatenate([qbrp_ref[:, h * LANE:(h + 1) * LANE] for h in range(B_HEADS)], axis=0)

    def masked(s, ok):
        w = s.shape[1]
        return jnp.where(ok[None], s.reshape(B_HEADS, tq, w), NEG_INF).reshape(rows, w)

    def values(kv):
        lane = lax.broadcasted_iota(I32, kv.shape, 1)
        return jnp.where(lane < B_HD, jnp.ones_like(kv), kv)

    qpos_s = pos0 + lax.broadcasted_iota(I32, (tq, tk), 0)
    c_s = lax.broadcasted_iota(I32, (tq, tk), 1)

    def sel_step(j, carry):
        m, acc = carry
        start = pl.multiple_of(j * tk, tk)
        kv = kvs_ref[pl.ds(start, tk), :]
        ok = (_expand_blocks(chosen, start, tk) > 0.5) & (start + c_s <= qpos_s)
        s = masked(_dot_nt(qr, kv), ok)
        m_new = jnp.maximum(m, jnp.max(s, axis=-1, keepdims=True))
        p = jnp.exp2(s - m_new)
        acc = jnp.exp2(m - m_new) * acc + _dot(p.astype(BF16), values(kv))
        return m_new, acc

    n_chunks = (pos0 + tq - 1) // tk + 1
    init = (jnp.full((rows, 1), NEG_INF, F32), jnp.zeros((rows, LANE), F32))
    _, acc = lax.fori_loop(0, n_chunks, sel_step, init)
    o_s = acc / acc[:, 0:1]

    span = WINDOW + tq
    start = pl.multiple_of(jnp.maximum(pos0 - WINDOW, 0), tq)
    kv = kvw_ref[pl.ds(start, span), :]
    qpos_w = pos0 + lax.broadcasted_iota(I32, (tq, span), 0)
    tok_w = start + lax.broadcasted_iota(I32, (tq, span), 1)
    s = masked(_dot_nt(qr, kv), (tok_w <= qpos_w) & (tok_w > qpos_w - WINDOW))
    p = jnp.exp2(s - jnp.max(s, axis=-1, keepdims=True))
    acc = _dot(p.astype(BF16), values(kv))
    o_w = acc / acc[:, 0:1]

    o_ref[...] = _nsa_combine(g_ref[...], o_c, o_s, o_w, tq).astype(BF16)


def _nsa_prompt(qbp, qbrp, kcvc, kvsb, kvwb, g, nbatch, seq, tq, tk):
    nq = seq // tq
    nb = seq // NSA_BLOCK
    assert tk % tq == 0 and seq % tk == 0 and seq >= WINDOW + tq and WINDOW % tq == 0
    tile = lambda w: pl.BlockSpec((tq, w), lambda b, i: (b * nq + i, 0))
    return pl.pallas_call(
        functools.partial(_nsa_prompt_kernel, tq=tq, tk=tk),
        grid=(nbatch, nq),
        in_specs=[tile(4 * LANE), tile(4 * LANE), pl.BlockSpec((None, nb, LANE), lambda b, i: (b, 0, 0)),
                  pl.BlockSpec((seq, LANE), lambda b, i: (b, 0)), pl.BlockSpec((seq, LANE), lambda b, i: (b, 0)),
                  tile(LANE)],
        out_specs=tile(B_W),
        out_shape=jax.ShapeDtypeStruct((nbatch * seq, B_W), BF16),
        compiler_params=_params("parallel", "arbitrary"),
        name="nsa_prompt",
    )(qbp, qbrp, kcvc, kvsb, kvwb, g)


def _nsa_sample_body(qbp_ref, qbrp_ref, g_ref, cn_ref, sn_ref, wn_ref, win_ref, pe_ref, w_ref, cmp_pages, slc_pages,
                     o_ref, msum_ref, *, n_new, past, nbp):
    n_pages = len(cmp_pages)
    tq = TOK_PAD
    rows = B_HEADS * tq
    page_rows = cmp_pages[0].shape[0]
    bpp = page_rows // NSA_BLOCK
    nb_past = n_pages * bpp

    for p, pg in enumerate(cmp_pages):
        msum_ref[p * bpp:(p + 1) * bpp, :] = jnp.sum(pg[...].reshape(bpp, NSA_BLOCK, LANE), axis=1)
    msum_ref[nb_past:nb_past + 1, :] = jnp.sum(cn_ref[...], axis=0, keepdims=True)
    msum_ref[nb_past + 1:nbp, :] = jnp.zeros((nbp - nb_past - 1, LANE), F32)
    mblk = msum_ref[...] * (1.0 / NSA_BLOCK) + jnp.mean(pe_ref[...], axis=0, keepdims=True)
    kcvc = jnp.dot(mblk, w_ref[...], preferred_element_type=F32, precision=lax.Precision.HIGHEST)

    qc = jnp.concatenate([qbp_ref[:, h * LANE:(h + 1) * LANE] for h in range(B_HEADS)], axis=0)
    o_c, chosen = _nsa_compressed_branch(qc, kcvc, past, tq)
    qr = jnp.concatenate([qbrp_ref[:, h * LANE:(h + 1) * LANE] for h in range(B_HEADS)], axis=0)

    def masked(s, ok):
        w = s.shape[1]
        return jnp.where(ok[None], s.reshape(B_HEADS, tq, w), NEG_INF).reshape(rows, w)

    tok_n = lax.broadcasted_iota(I32, (tq, NEW_ROWS), 0)
    j_n = lax.broadcasted_iota(I32, (tq, NEW_ROWS), 1)
    new_ok = (j_n <= tok_n) & (j_n < n_new)

    def attend(score_tiles, value_tiles):
        m = None
        for s in score_tiles:
            mx = jnp.max(s, axis=-1, keepdims=True)
            m = mx if m is None else jnp.maximum(m, mx)
        l = jnp.zeros((rows, 1), F32)
        acc = jnp.zeros((rows, LANE), F32)
        for s, v in zip(score_tiles, value_tiles):
            p = jnp.exp2(s - m)
            l = l + jnp.sum(p, axis=-1, keepdims=True)
            acc = acc + _dot(p.astype(BF16), v)
        return acc / l

    tiles, vals = [], []
    for p, pg in enumerate(slc_pages):
        kv = pg[...].astype(BF16)
        ok = _expand_blocks(chosen, p * page_rows, page_rows) > 0.5
        tiles.append(masked(_dot_nt(qr, kv), ok))
        vals.append(kv)
    kv = sn_ref[...].astype(BF16)
    ok = (_expand_blocks(chosen, past, NEW_ROWS) > 0.5) & new_ok
    tiles.append(masked(_dot_nt(qr, kv), ok))
    vals.append(kv)
    o_s = attend(tiles, vals)

    wb = win_ref.shape[0]
    kv = win_ref[...].astype(BF16)
    r_w = lax.broadcasted_iota(I32, (tq, wb), 1)
    t_w = lax.broadcasted_iota(I32, (tq, wb), 0)
    tiles = [masked(_dot_nt(qr, kv), r_w > t_w + (wb - WINDOW))]
    vals = [kv]
    kv = wn_ref[...].astype(BF16)
    tiles.append(masked(_dot_nt(qr, kv), new_ok))
    vals.append(kv)
    o_w = attend(tiles, vals)

    o_ref[...] = _nsa_combine(g_ref[...], o_c, o_s, o_w, tq).astype(BF16)


N_DIFF_SEQ_IN = 5
N_NSA_SEQ_IN = 9


def _sample_mixers_kernel(pt_ref, *refs, n_pages, n_new, past, nbp, lam_init):
    del pt_ref
    diff_in = refs[:N_DIFF_SEQ_IN]
    nsa_in = refs[N_DIFF_SEQ_IN:N_DIFF_SEQ_IN + N_NSA_SEQ_IN]
    pages = refs[N_DIFF_SEQ_IN + N_NSA_SEQ_IN:]
    k_pages, v_pages, cmp_pages, slc_pages = (pages[i * n_pages:(i + 1) * n_pages] for i in range(4))
    oa_ref, ob_ref, msum_ref = pages[4 * n_pages:]
    _diff_sample_body(*diff_in, k_pages, v_pages, oa_ref, n_new=n_new, lam_init=lam_init)
    _nsa_sample_body(*nsa_in, cmp_pages, slc_pages, ob_ref, msum_ref, n_new=n_new, past=past, nbp=nbp)


def _sample_mixers(pt_flat, lw, subw, q8, kn, vn, qbp8, qbrp8, g8, cn, sn, wn, state_win, pe_cat, w_bd, cache_k,
                   cache_v, cache_cmp, cache_slc, layer, lam_init, n_pages, n_new, past):
    nseq = q8.shape[0]
    new_rows = n_new * A_HEADS
    assert new_rows % 8 == 0
    wb = state_win.shape[2]
    nb = -(-(past + n_new) // NSA_BLOCK)
    nbp = -(-nb // 16) * 16

    def page_specs(cache):
        rows = cache.shape[2]
        return [pl.BlockSpec((None, None, rows, LANE), lambda b, pt, p=p: (layer, pt[b * n_pages + p], 0, 0))
                for p in range(n_pages)]

    seq_spec = lambda r, w: pl.BlockSpec((None, r, w), lambda b, pt: (b, 0, 0))
    const_spec = lambda r, w: pl.BlockSpec((r, w), lambda b, pt: (0, 0))
    caches = (cache_k, cache_v, cache_cmp, cache_slc)
    grid_spec = pltpu.PrefetchScalarGridSpec(
        num_scalar_prefetch=1,
        grid=(nseq,),
        in_specs=[const_spec(4, A_HD), const_spec(1, LANE), seq_spec(TOK_PAD, A_W), seq_spec(new_rows, LANE),
                  seq_spec(new_rows, LANE),
                  seq_spec(TOK_PAD, 4 * LANE), seq_spec(TOK_PAD, 4 * LANE), seq_spec(TOK_PAD, LANE),
                  seq_spec(NEW_ROWS, LANE), seq_spec(NEW_ROWS, LANE), seq_spec(NEW_ROWS, LANE),
                  pl.BlockSpec((None, None, wb, LANE), lambda b, pt: (layer, b, 0, 0)),
                  const_spec(NSA_BLOCK, LANE), const_spec(LANE, LANE)]
        + [s for c in caches for s in page_specs(c)],
        out_specs=[seq_spec(TOK_PAD, A_W), seq_spec(TOK_PAD, B_W)],
        scratch_shapes=[pltpu.VMEM((nbp, LANE), F32)],
    )
    return pl.pallas_call(
        functools.partial(_sample_mixers_kernel, n_pages=n_pages, n_new=n_new, past=past, nbp=nbp,
                          lam_init=lam_init),
        grid_spec=grid_spec,
        out_shape=[jax.ShapeDtypeStruct((nseq, TOK_PAD, A_W), BF16), jax.ShapeDtypeStruct((nseq, TOK_PAD, B_W), BF16)],
        compiler_params=_params("parallel"),
        name="sample_mixers",
    )(pt_flat, lw, subw, q8, kn, vn, qbp8, qbrp8, g8, cn, sn, wn, state_win, pe_cat, w_bd,
      *[c for c in caches for _ in range(n_pages)])


def _conv_prompt_kernel(c_ref, halo_ref, b_ref, w_ref, o_ref):
    i = pl.program_id(1)
    c = c_ref[...]
    rows = c.shape[0]
    halo = jnp.where(i > 0, halo_ref[...], 0.0)
    r = lax.broadcasted_iota(I32, c.shape, 0)
    c1 = jnp.where(r == 0, halo[7:8], pltpu.roll(c, 1, 0))
    c2 = jnp.where(r == 0, halo[6:7], jnp.where(r == 1, halo[7:8], pltpu.roll(c, 2, 0)))
    del rows
    z = w_ref[0:1] * c2 + w_ref[1:2] * c1 + w_ref[2:3] * c
    o_ref[...] = (b_ref[...] * z).astype(BF16)


def _conv_prompt(cin, bgate, conv_w, nbatch, seq, tm):
    nt = seq // tm
    hb = tm // 8
    tile = pl.BlockSpec((tm, C_W), lambda b, i: (b * nt + i, 0))
    return pl.pallas_call(
        _conv_prompt_kernel,
        grid=(nbatch, nt),
        in_specs=[tile, pl.BlockSpec((8, C_W), lambda b, i: (jnp.maximum((b * nt + i) * hb - 1, 0), 0)), tile,
                  pl.BlockSpec((CONV_W, C_W), lambda b, i: (0, 0))],
        out_specs=tile,
        out_shape=jax.ShapeDtypeStruct((nbatch * seq, C_W), BF16),
        compiler_params=_params("parallel", "parallel"),
        name="conv_prompt",
    )(cin, cin, bgate, conv_w)


def _conv_sample_kernel(c0_ref, c1_ref, c2_ref, b_ref, w_ref, o_ref):
    z = w_ref[0:1] * c0_ref[...] + w_ref[1:2] * c1_ref[...] + w_ref[2:3] * c2_ref[...]
    o_ref[...] = (b_ref[...] * z).astype(BF16)


def _conv_sample(c0, c1, c2, bgate, conv_w):
    return pl.pallas_call(
        _conv_sample_kernel,
        out_shape=jax.ShapeDtypeStruct(c0.shape, BF16),
        name="conv_sample",
    )(c0, c1, c2, bgate, conv_w)


def _mix_out_kernel(x_ref, oa_ref, ob_ref, oc_ref, wout_ref, nw_ref, wq_ref, x_out_ref, q_out_ref):
    y = (_dot(oa_ref[...], wout_ref[0:A_W]) + _dot(ob_ref[...], wout_ref[A_W:A_W + B_W])
         + _dot(oc_ref[...], wout_ref[A_W + B_W:A_W + B_W + C_W]))
    x = x_ref[...] + _rms(y, nw_ref[NORM_MIX_POST:NORM_MIX_POST + 1])
    x_out_ref[...] = x
    h = _rms(x, nw_ref[NORM_X_PRE:NORM_X_PRE + 1]).astype(BF16)
    q_out_ref[...] = (_dot(h, wq_ref[...]) * Q_SCALE).astype(BF16)


def _mix_out(x, mixed, w_out, nw8, w_q, tm):
    n, d = x.shape
    row = lambda w: pl.BlockSpec((tm, w), lambda i: (i, 0))
    full = lambda a: pl.BlockSpec(a.shape, lambda i: (0, 0))
    return pl.pallas_call(
        _mix_out_kernel,
        grid=(n // tm,),
        in_specs=[row(d)] + [row(a.shape[1]) for a in mixed] + [full(w_out), full(nw8), full(w_q)],
        out_specs=[row(d), row(X_W)],
        out_shape=[jax.ShapeDtypeStruct((n, d), F32), jax.ShapeDtypeStruct((n, X_W), BF16)],
        compiler_params=_params("parallel"),
        name="mix_out",
    )(x, *mixed, w_out, nw8, w_q)


def _cross_attn_kernel(q_ref, mkt_ref, mvt_ref, o_ref):
    q = q_ref[...]
    ts = q.shape[0]
    lane = lax.broadcasted_iota(I32, q.shape, 1)
    zero = jnp.zeros_like(q)
    qs = jnp.concatenate(
        [jnp.where((lane >= h * X_HD) & (lane < (h + 1) * X_HD), q, zero) for h in range(X_HEADS)], axis=0)
    s = _dot(qs, mkt_ref[...].astype(BF16))
    e = jnp.exp2(s - jnp.max(s, axis=-1, keepdims=True))
    p = e / jnp.sum(e, axis=-1, keepdims=True)
    o = _dot_nt(p.astype(BF16), mvt_ref[...].astype(BF16))
    out = jnp.zeros((ts, X_W), F32)
    for h in range(X_HEADS):
        out = out + jnp.where((lane >= h * X_HD) & (lane < (h + 1) * X_HD), o[h * ts:(h + 1) * ts], 0.0)
    o_ref[...] = out.astype(BF16)


def _cross_attn(q, mkt, mvt, ts, v_block=0, b_off=0):
    nbatch, seq, _ = q.shape
    n_mem = mkt.shape[2]
    return pl.pallas_call(
        _cross_attn_kernel,
        grid=(nbatch, seq // ts),
        in_specs=[pl.BlockSpec((None, ts, X_W), lambda b, i: (b, i, 0)),
                  pl.BlockSpec((None, X_W, n_mem), lambda b, i: (b_off + b, 0, 0)),
                  pl.BlockSpec((None, X_W, n_mem), lambda b, i: (b_off + b, v_block, 0))],
        out_specs=pl.BlockSpec((None, ts, X_W), lambda b, i: (b, i, 0)),
        out_shape=jax.ShapeDtypeStruct((nbatch, seq, X_W), BF16),
        compiler_params=_params("parallel", "parallel"),
        name="cross_attn",
    )(q, mkt, mvt)


def _mem_kv_kernel(m_ref, nw_ref, wt_ref, o_ref):
    h = _rms(m_ref[...], nw_ref[...]).astype(BF16)
    o_ref[...] = _dot_nt(wt_ref[...], h)


def _mem_kv(mem, nw, w_kv_t):
    nbatch, n_mem, d = mem.shape
    return pl.pallas_call(
        _mem_kv_kernel,
        grid=(nbatch,),
        in_specs=[pl.BlockSpec((None, n_mem, d), lambda b: (b, 0, 0)), pl.BlockSpec((1, d), lambda b: (0, 0)),
                  pl.BlockSpec(w_kv_t.shape, lambda b: (0, 0))],
        out_specs=pl.BlockSpec((None, 2 * X_W, n_mem), lambda b: (b, 0, 0)),
        out_shape=jax.ShapeDtypeStruct((nbatch, 2 * X_W, n_mem), F32),
        compiler_params=_params("parallel"),
        name="mem_kv",
    )(mem, nw, w_kv_t)


def _ffn_kernel(x_ref, o_ref, wo_ref, nw_ref, wg_ref, wu_ref, wd_ref, y_ref, x2_ref, h_ref, acc_ref):
    j = pl.program_id(1)

    @pl.when(j == 0)
    def _():
        x2 = x_ref[...] + _rms(_dot(o_ref[...], wo_ref[...]), nw_ref[NORM_X_POST:NORM_X_POST + 1])
        x2_ref[...] = x2
        h_ref[...] = _rms(x2, nw_ref[NORM_FFN_PRE:NORM_FFN_PRE + 1]).astype(BF16)
        acc_ref[...] = jnp.zeros_like(acc_ref)

    h = h_ref[...]
    g = _dot(h, wg_ref[...])
    u = _dot(h, wu_ref[...])
    a = (g * jax.nn.sigmoid(g)) * u
    acc_ref[...] += _dot(a.astype(BF16), wd_ref[...])

    @pl.when(j == pl.num_programs(1) - 1)
    def _():
        y_ref[...] = x2_ref[...] + _rms(acc_ref[...], nw_ref[NORM_FFN_POST:NORM_FFN_POST + 1])


def _ffn(x, o, w_o, nw8, w_g, w_u, w_d, tm, tf):
    n, d = x.shape
    dff = w_g.shape[1]
    row = lambda w: pl.BlockSpec((tm, w), lambda i, j: (i, 0))
    return pl.pallas_call(
        _ffn_kernel,
        grid=(n // tm, dff // tf),
        in_specs=[row(d), row(X_W), pl.BlockSpec(w_o.shape, lambda i, j: (0, 0)),
                  pl.BlockSpec(nw8.shape, lambda i, j: (0, 0)),
                  pl.BlockSpec((d, tf), lambda i, j: (0, j)), pl.BlockSpec((d, tf), lambda i, j: (0, j)),
                  pl.BlockSpec((tf, d), lambda i, j: (j, 0))],
        out_specs=row(d),
        out_shape=jax.ShapeDtypeStruct((n, d), F32),
        scratch_shapes=[pltpu.VMEM((tm, d), F32), pltpu.VMEM((tm, d), BF16), pltpu.VMEM((tm, d), F32)],
        compiler_params=_params("parallel", "arbitrary"),
        name="ffn",
    )(x, o, w_o, nw8, w_g, w_u, w_d)


def _rope_tables(pos):
    inv = ROPE_THETA ** (-jnp.arange(ROPE_HALF, dtype=F32) / ROPE_HALF)
    ang = pos.astype(F32)[:, None] * inv[None, :]
    cos, sin = jnp.cos(ang), jnp.sin(ang)
    n = pos.shape[0]
    rest = A_HD - 2 * ROPE_HALF
    z8 = jnp.zeros((n, ROPE_HALF), F32)
    c = jnp.concatenate([cos, cos, jnp.ones((n, rest), F32)], axis=1)
    s1 = jnp.concatenate([-sin, z8, jnp.zeros((n, rest), F32)], axis=1)
    s2 = jnp.concatenate([z8, sin, jnp.zeros((n, rest), F32)], axis=1)
    return jnp.concatenate([c, c, s1, s1, s2, s2], axis=1)


def _pad_rows(t, rows):
    return jnp.pad(t, ((0, 0), (0, rows - t.shape[1]), (0, 0)))


def kernel(x_prompt, x_sample, cache_diff_k, cache_diff_v, cache_nsa_cmp, cache_nsa_slc, state_nsa_win, state_conv,
           cache_mem_k, cache_mem_v, page_table, mem_prompt, norm_w, w_in, w_out, diff_lambda, diff_subln, nsa_pe,
           nsa_w_cmp, conv_w, w_q_mem, w_kv_mem, w_o_mem, w_ffn_up, w_ffn_down):
    depth = w_in.shape[0]
    nbp, seq, d = x_prompt.shape
    nbs, sseq, _ = x_sample.shape
    n_p, n_s = nbp * seq, nbs * sseq
    n_pool, page_rows = cache_diff_k.shape[1], cache_diff_k.shape[2]
    n_pages = page_table.shape[1]
    past = n_pages * page_rows
    n_mem = mem_prompt.shape[1]
    dff = w_ffn_down.shape[1]
    assert seq % 256 == 0 and seq >= WINDOW and sseq <= TOK_PAD and d == 1024

    tm_p = _pick_tile(seq, 512)
    tm_s = _pick_tile(n_s, 512)
    tf = _pick_tile(dff, 1408, 128)

    split = 2176
    w_in_p = jnp.concatenate(
        [w_in[:, :, :split], w_in[:, :, split + 3 * B_HEADS:], w_in[:, :, split:split + 3 * B_HEADS],
         jnp.zeros((depth, d, IN_COLS_PADDED - w_in.shape[2]), w_in.dtype)], axis=2).astype(BF16)
    w_out_b = w_out.astype(BF16)
    w_q_b = w_q_mem.astype(BF16)
    w_kv_t = jnp.swapaxes(w_kv_mem, 1, 2).astype(BF16)
    mkt_s, mvt_s = (jnp.transpose(t, (0, 1, 3, 4, 2)).reshape(depth * nbs, X_W, n_mem)
                    for t in (cache_mem_k, cache_mem_v))
    w_o_b = w_o_mem.astype(BF16)
    w_g_b = w_ffn_up[:, :, :dff].astype(BF16)
    w_u_b = w_ffn_up[:, :, dff:].astype(BF16)
    w_d_b = w_ffn_down.astype(BF16)
    nw8 = jnp.pad(norm_w, ((0, 0), (0, 1), (0, 0)))
    pe_cat = jnp.concatenate([nsa_pe[:, 0], nsa_pe[:, 1]], axis=-1)
    zb = jnp.zeros((depth, B_HD, B_HD), F32)
    w_bd = jnp.concatenate([jnp.concatenate([nsa_w_cmp[:, 0], zb], axis=2),
                            jnp.concatenate([zb, nsa_w_cmp[:, 1]], axis=2)], axis=1)
    subw = diff_subln.reshape(depth, 1, 2 * A_HD)

    rope_p = _rope_tables(jnp.arange(seq, dtype=I32))
    rope_s = _rope_tables(jnp.tile(past + jnp.arange(sseq, dtype=I32), nbs))
    pt_flat = page_table.reshape(-1).astype(I32)
    ck = cache_diff_k.reshape(depth, n_pool, page_rows * A_HEADS, 2 * A_HD)
    cv = cache_diff_v.reshape(depth, n_pool, page_rows * A_HEADS, 2 * A_HD)

    xp = x_prompt.reshape(n_p, d)
    xs = x_sample.reshape(n_s, d)
    outs = [[] for _ in range(14)]

    def sample3(t, rows):
        return _pad_rows(t.reshape(nbs, sseq, t.shape[1]), rows)

    for l in range(depth):
        lam_init = 0.8 - 0.6 * math.exp(-0.3 * l)
        nw_pre = norm_w[l, NORM_MIX_PRE:NORM_MIX_PRE + 1]

        (qa, ka, kab, va, vab, qbp, qbrp, kvc, kvs, kvsb, kvw, kvwb, g, cin, bgate) = _proj_in(
            xp, nw_pre, w_in_p[l], rope_p, tm_p)
        oa_p = _diff_prompt(qa, kab, vab, diff_lambda[l], subw[l], lam_init, nbp, seq, 512, 512)
        kcvc = _nsa_compress(kvc, pe_cat[l], w_bd[l], nbp, seq)
        ob_p = _nsa_prompt(qbp, qbrp, kcvc, kvsb, kvwb, g, nbp, seq, 256, 512)
        oc_p = _conv_prompt(cin, bgate, conv_w[l], nbp, seq, _pick_tile(seq, 512))
        xp, qx_p = _mix_out(xp, (oa_p, ob_p, oc_p), w_out_b[l], nw8[l], w_q_b[l], tm_p)
        mkv_t = _mem_kv(mem_prompt, norm_w[l, NORM_MEM:NORM_MEM + 1], w_kv_t[l])
        ox_p = _cross_attn(qx_p.reshape(nbp, seq, X_W), mkv_t, mkv_t, _pick_tile(seq, 512), v_block=1)
        xp = _ffn(xp, ox_p.reshape(n_p, X_W), w_o_b[l], nw8[l], w_g_b[l], w_u_b[l], w_d_b[l], tm_p, tf)
        mk_p, mv_p = (jnp.transpose(t.reshape(nbp, X_HEADS, X_HD, n_mem), (0, 3, 1, 2))
                      for t in (mkv_t[:, :X_W], mkv_t[:, X_W:]))

        (qa_s, ka_s, _, va_s, _, qbp_s, qbrp_s, kvc_s, kvs_s, _, kvw_s, _, g_s, cin_s, bgate_s) = _proj_in(
            xs, nw_pre, w_in_p[l], rope_s, tm_s)
        new_heads = lambda t: t.reshape(nbs, sseq * A_HEADS, 2 * A_HD)
        oa_s, ob_s = _sample_mixers(
            pt_flat, diff_lambda[l], subw[l], sample3(qa_s, TOK_PAD), new_heads(ka_s), new_heads(va_s),
            sample3(qbp_s, TOK_PAD), sample3(qbrp_s, TOK_PAD), sample3(g_s, TOK_PAD), sample3(kvc_s, NEW_ROWS),
            sample3(kvs_s, NEW_ROWS), sample3(kvw_s, NEW_ROWS), state_nsa_win, pe_cat[l], w_bd[l], ck, cv,
            cache_nsa_cmp, cache_nsa_slc, l, lam_init, n_pages, sseq, past)
        c_all = jnp.concatenate([state_conv[l], cin_s.reshape(nbs, sseq, C_W)], axis=1)
        oc_s = _conv_sample(*[c_all[:, j:j + sseq].reshape(n_s, C_W) for j in range(CONV_W)], bgate_s, conv_w[l])
        xs, qx_s = _mix_out(xs, (oa_s[:, :sseq].reshape(n_s, A_W), ob_s[:, :sseq].reshape(n_s, B_W), oc_s),
                            w_out_b[l], nw8[l], w_q_b[l], tm_s)
        ox_s = _cross_attn(_pad_rows(qx_s.reshape(nbs, sseq, X_W), TOK_PAD), mkt_s, mvt_s, TOK_PAD, b_off=l * nbs)
        xs = _ffn(xs, ox_s[:, :sseq].reshape(n_s, X_W), w_o_b[l], nw8[l], w_g_b[l], w_u_b[l], w_d_b[l], tm_s, tf)

        win_p = kvw.reshape(nbp, seq, 2 * B_HD)[:, seq - WINDOW:]
        cin_p = cin.reshape(nbp, seq, C_W)
        win_s = kvw_s.reshape(nbs, sseq, 2 * B_HD)
        layer_out = (
            ka.reshape(nbp, seq, A_HEADS, 2 * A_HD), va.reshape(nbp, seq, A_HEADS, 2 * A_HD),
            kvc.reshape(nbp, seq, 2 * B_HD), kvs.reshape(nbp, seq, 2 * B_HD), win_p,
            cin_p[:, seq - (CONV_W - 1):], mk_p, mv_p,
            ka_s.reshape(nbs, sseq, A_HEADS, 2 * A_HD), va_s.reshape(nbs, sseq, A_HEADS, 2 * A_HD),
            kvc_s.reshape(nbs, sseq, 2 * B_HD), kvs_s.reshape(nbs, sseq, 2 * B_HD), win_s, c_all[:, sseq:])
        for acc, t in zip(outs, layer_out):
            acc.append(t)

    stacked = [jnp.stack(t) for t in outs]
    stacked[12] = jnp.concatenate([state_nsa_win, stacked[12]], axis=2)[:, :, sseq:]
    return (xp.reshape(nbp, seq, d), xs.reshape(nbs, sseq, d), *stacked)
```

```python
import functools
import math

import jax
import jax.numpy as jnp
from jax import lax
from jax.experimental import pallas as pl
from jax.experimental.pallas import tpu as pltpu

F32 = jnp.float32
BF16 = jnp.bfloat16
I32 = jnp.int32

A_HEADS = 4
A_HD = 64
A_W = A_HEADS * 2 * A_HD
B_HEADS = 4
B_HD = 64
B_W = B_HEADS * B_HD
NSA_BLOCK = 64
NSA_TOPK = 16
WINDOW = 512
C_W = 256
CONV_W = 3
X_HEADS = 4
X_HD = 64
X_W = X_HEADS * X_HD
ROPE_THETA = 500000.0
ROPE_HALF = (A_HD // 4) // 2
EPS = 1e-6
NEG_INF = -1e30
FORCE_SCORE = 1e4
Q_SCALE = A_HD ** -0.5 * math.log2(math.e)
LANE = 128
IN_COLS_PADDED = 3072
NEW_ROWS = 16
TOK_PAD = 8
VMEM_LIMIT = 56 * 1024 * 1024

NORM_MIX_PRE, NORM_MIX_POST, NORM_X_PRE, NORM_X_POST, NORM_FFN_PRE, NORM_FFN_POST, NORM_MEM = range(7)

NT_DIMS = (((1,), (1,)), ((), ()))


def _params(*sem):
    return pltpu.CompilerParams(dimension_semantics=sem, vmem_limit_bytes=VMEM_LIMIT)


def _pick_tile(n, target, mult=16):
    best = None
    for t in range(mult, min(n, target) + 1, mult):
        if n % t == 0:
            best = t
    assert best is not None, (n, target)
    return best


def _rms(x, w):
    return x * lax.rsqrt(jnp.mean(x * x, axis=-1, keepdims=True) + EPS) * w


def _dot(a, b):
    return jnp.dot(a, b, preferred_element_type=F32)


def _dot_nt(a, b):
    return lax.dot_general(a, b, NT_DIMS, preferred_element_type=F32)


def _rope(z, c, s1, s2):
    outs = []
    for j in range(z.shape[1] // LANE):
        t = z[:, j * LANE:(j + 1) * LANE]
        outs.append(t * c + pltpu.roll(t, LANE - ROPE_HALF, 1) * s1 + pltpu.roll(t, ROPE_HALF, 1) * s2)
    return outs[0] if len(outs) == 1 else jnp.concatenate(outs, axis=1)


def _softmax_update(s, v, m, l, acc):
    m_new = jnp.maximum(m, jnp.max(s, axis=-1, keepdims=True))
    p = jnp.exp2(s - m_new)
    alpha = jnp.exp2(m - m_new)
    l = alpha * l + jnp.sum(p, axis=-1, keepdims=True)
    acc = alpha * acc + _dot(p.astype(BF16), v)
    return m_new, l, acc


def _proj_in_kernel(x_ref, nw_ref, w_ref, rope_ref, qa_ref, ka_ref, kab_ref, va_ref, vab_ref, qbp_ref, qbrp_ref,
                    kvc_ref, kvs_ref, kvsb_ref, kvw_ref, kvwb_ref, g_ref, cin_ref, bgate_ref):
    h = _rms(x_ref[...], nw_ref[...]).astype(BF16)
    c = rope_ref[:, 0:LANE]
    s1 = rope_ref[:, LANE:2 * LANE]
    s2 = rope_ref[:, 2 * LANE:3 * LANE]

    def mm(a, b):
        return _dot(h, w_ref[:, a:b])

    qa_ref[...] = (_rope(mm(0, 512), c, s1, s2) * Q_SCALE).astype(BF16)
    zk = _rope(mm(512, 1024), c, s1, s2)
    def store_head_rows(ref, z):
        for hd in range(A_HEADS):
            ref[pl.ds(hd, z.shape[0], stride=A_HEADS), :] = z[:, hd * LANE:(hd + 1) * LANE]

    store_head_rows(ka_ref, zk)
    kab_ref[...] = zk.astype(BF16)
    zv = mm(1024, 1536)
    store_head_rows(va_ref, zv)
    vab_ref[...] = zv.astype(BF16)

    zb = mm(1536, 1792) * Q_SCALE
    zbr = _rope(zb, c, s1, s2)
    first = lax.broadcasted_iota(I32, (zb.shape[0], LANE), 1) < B_HD

    def pad_heads(z):
        outs = []
        for j in range(2):
            t = z[:, j * LANE:(j + 1) * LANE]
            outs.append(jnp.where(first, t, 0.0))
            outs.append(jnp.where(first, pltpu.roll(t, B_HD, 1), 0.0))
        return jnp.concatenate(outs, axis=1)

    qbp_ref[...] = pad_heads(zb).astype(BF16)
    qbrp_ref[...] = pad_heads(zbr).astype(BF16)

    zkv = mm(1792, 2176)
    kvc_ref[...] = zkv[:, 0:LANE]
    ckv = jnp.where(first, c, 1.0)
    s1kv = jnp.where(first, s1, 0.0)
    s2kv = jnp.where(first, s2, 0.0)
    ks = _rope(zkv[:, LANE:2 * LANE], ckv, s1kv, s2kv)
    kvs_ref[...] = ks
    kvsb_ref[...] = ks.astype(BF16)
    kw = _rope(zkv[:, 2 * LANE:3 * LANE], ckv, s1kv, s2kv)
    kvw_ref[...] = kw
    kvwb_ref[...] = kw.astype(BF16)

    zc = mm(2176, 2944)
    cin_ref[...] = zc[:, 2 * C_W:3 * C_W] * zc[:, 0:C_W]
    bgate_ref[...] = zc[:, C_W:2 * C_W]
    g_ref[...] = mm(2944, 3072)


def _proj_in(x, nw, w, rope, tm):
    n, d = x.shape
    period_tiles = rope.shape[0] // tm
    assert rope.shape[0] % tm == 0
    row = lambda wdt: pl.BlockSpec((tm, wdt), lambda i: (i, 0))
    rope_spec = pl.BlockSpec((tm, 3 * LANE), lambda i: (i % period_tiles, 0))
    head_rows = pl.BlockSpec((tm * A_HEADS, LANE), lambda i: (i, 0))
    widths = [(512, BF16), (512, F32), (512, BF16), (512, F32), (512, BF16), (512, BF16), (512, BF16),
              (128, F32), (128, F32), (128, BF16), (128, F32), (128, BF16), (128, F32), (C_W, F32), (C_W, F32)]
    return pl.pallas_call(
        _proj_in_kernel,
        grid=(n // tm,),
        in_specs=[row(d), pl.BlockSpec((1, d), lambda i: (0, 0)),
                  pl.BlockSpec((d, IN_COLS_PADDED), lambda i: (0, 0)), rope_spec],
        out_specs=[head_rows if j in (1, 3) else row(wd) for j, (wd, _) in enumerate(widths)],
        out_shape=[jax.ShapeDtypeStruct((n * A_HEADS, LANE) if j in (1, 3) else (n, wd), dt)
                   for j, (wd, dt) in enumerate(widths)],
        compiler_params=_params("parallel"),
        name="proj_in",
    )(x, nw, w, rope)


def _diff_lambda(lw):
    a = jnp.sum(lw[0:1] * lw[1:2], axis=-1, keepdims=True)
    b = jnp.sum(lw[2:3] * lw[3:4], axis=-1, keepdims=True)
    return jnp.exp(a) - jnp.exp(b)


def _diff_finish(o1, o2, lam, subw, one_minus_init):
    d = o1 - lam * o2
    return _rms(d, subw) * one_minus_init


def _diff_prompt_kernel(lw_ref, q_ref, k_ref, v_ref, subw_ref, o_ref, *, tq, tk, lam_init):
    qi = pl.program_id(2)
    q = q_ref[...]
    lane = lax.broadcasted_iota(I32, (tq, LANE), 1)
    zero = jnp.zeros_like(q)
    qs = jnp.concatenate([jnp.where(lane < A_HD, q, zero), jnp.where(lane >= A_HD, q, zero)], axis=0)
    rows = 2 * tq

    def step(j, carry, diag):
        m, l, acc = carry
        start = pl.multiple_of(j * tk, tk)
        s = _dot_nt(qs, k_ref[pl.ds(start, tk), :])
        if diag:
            r = lax.broadcasted_iota(I32, (rows, tk), 0) & (tq - 1)
            cidx = lax.broadcasted_iota(I32, (rows, tk), 1)
            s = jnp.where(start + cidx <= qi * tq + r, s, NEG_INF)
        return _softmax_update(s, v_ref[pl.ds(start, tk), :], m, l, acc)

    n_full = (qi * tq) // tk
    init = (jnp.full((rows, 1), NEG_INF, F32), jnp.zeros((rows, 1), F32), jnp.zeros((rows, LANE), F32))
    carry = lax.fori_loop(0, n_full, lambda j, cr: step(j, cr, False), init)
    m, l, acc = step(n_full, carry, True)
    o = acc / l
    lam = _diff_lambda(lw_ref[...]) + lam_init
    o_ref[...] = _diff_finish(o[:tq], o[tq:], lam, subw_ref[...], 1.0 - lam_init).astype(BF16)


def _diff_prompt(qa, kab, vab, lw, subw, lam_init, nbatch, seq, tq, tk):
    nq = seq // tq
    assert tk % tq == 0 and seq % tk == 0
    return pl.pallas_call(
        functools.partial(_diff_prompt_kernel, tq=tq, tk=tk, lam_init=lam_init),
        grid=(nbatch, A_HEADS, nq),
        in_specs=[pl.BlockSpec((4, A_HD), lambda b, h, i: (0, 0)),
                  pl.BlockSpec((tq, LANE), lambda b, h, i: (b * nq + i, h)),
                  pl.BlockSpec((seq, LANE), lambda b, h, i: (b, h)),
                  pl.BlockSpec((seq, LANE), lambda b, h, i: (b, h)),
                  pl.BlockSpec((1, LANE), lambda b, h, i: (0, 0))],
        out_specs=pl.BlockSpec((tq, LANE), lambda b, h, i: (b * nq + i, h)),
        out_shape=jax.ShapeDtypeStruct((nbatch * seq, A_W), BF16),
        compiler_params=_params("parallel", "parallel", "arbitrary"),
        name="diff_prompt",
    )(lw, qa, kab, vab, subw)


def _diff_sample_body(lw_ref, subw_ref, q_ref, kn_ref, vn_ref, k_pages, v_pages, o_ref, *, n_new, lam_init):
    q = q_ref[...]
    lane = lax.broadcasted_iota(I32, (TOK_PAD, LANE), 1)
    pieces = []
    for h in range(A_HEADS):
        qh = q[:, h * LANE:(h + 1) * LANE]
        pieces += [jnp.where(lane < A_HD, qh, jnp.zeros_like(qh)), jnp.where(lane >= A_HD, qh, jnp.zeros_like(qh))]
    qs = jnp.concatenate(pieces, axis=0)
    rows = qs.shape[0]
    rows_per_head = 2 * TOK_PAD

    def own_head(width):
        r = lax.broadcasted_iota(I32, (rows, width), 0)
        c = lax.broadcasted_iota(I32, (rows, width), 1)
        return (c & (A_HEADS - 1)) == (r // rows_per_head), r, c

    ok_page, _, _ = own_head(k_pages[0].shape[0])
    s_past = [jnp.where(ok_page, _dot_nt(qs, kp[...].astype(BF16)), NEG_INF) for kp in k_pages]
    ok_new, r_new, c_new = own_head(n_new * A_HEADS)
    ok_new = ok_new & ((c_new // A_HEADS) <= (r_new & (TOK_PAD - 1)))
    s_new = jnp.where(ok_new, _dot_nt(qs, kn_ref[...].astype(BF16)), NEG_INF)
    m = jnp.max(s_new, axis=-1, keepdims=True)
    for s in s_past:
        m = jnp.maximum(m, jnp.max(s, axis=-1, keepdims=True))
    p_new = jnp.exp2(s_new - m)
    l = jnp.sum(p_new, axis=-1, keepdims=True)
    acc = _dot(p_new.astype(BF16), vn_ref[...].astype(BF16))
    for s, vp in zip(s_past, v_pages):
        p = jnp.exp2(s - m)
        l = l + jnp.sum(p, axis=-1, keepdims=True)
        acc = acc + _dot(p.astype(BF16), vp[...].astype(BF16))
    o = acc / l
    lam = _diff_lambda(lw_ref[...]) + lam_init
    outs = []
    for h in range(A_HEADS):
        o1 = o[(2 * h) * TOK_PAD:(2 * h + 1) * TOK_PAD]
        o2 = o[(2 * h + 1) * TOK_PAD:(2 * h + 2) * TOK_PAD]
        outs.append(_diff_finish(o1, o2, lam, subw_ref[...], 1.0 - lam_init))
    o_ref[...] = jnp.concatenate(outs, axis=1).astype(BF16)


def _nsa_compress_kernel(kv_ref, pe_ref, w_ref, o_ref, *, nb):
    kv = kv_ref[...]
    m = jnp.sum(kv.reshape(nb, NSA_BLOCK, LANE), axis=1) * (1.0 / NSA_BLOCK)
    m = m + jnp.mean(pe_ref[...], axis=0, keepdims=True)
    o_ref[...] = jnp.dot(m, w_ref[...], preferred_element_type=F32, precision=lax.Precision.HIGHEST)


def _nsa_compress(kvc, pe_cat, w_bd, nbatch, seq):
    nb = seq // NSA_BLOCK
    return pl.pallas_call(
        functools.partial(_nsa_compress_kernel, nb=nb),
        grid=(nbatch,),
        in_specs=[pl.BlockSpec((seq, LANE), lambda b: (b, 0)), pl.BlockSpec((NSA_BLOCK, LANE), lambda b: (0, 0)),
                  pl.BlockSpec((LANE, LANE), lambda b: (0, 0))],
        out_specs=pl.BlockSpec((None, nb, LANE), lambda b: (b, 0, 0)),
        out_shape=jax.ShapeDtypeStruct((nbatch, nb, LANE), F32),
        compiler_params=_params("parallel"),
        name="nsa_compress",
    )(kvc, pe_cat, w_bd)


def _select_blocks_transposed(qc, kc, pos0, tq):
    nb = kc.shape[0]
    rows = B_HEADS * tq
    sc = _dot_nt(kc, qc)
    qpos = pos0 + (lax.broadcasted_iota(I32, (nb, rows), 1) & (tq - 1))
    blk = lax.broadcasted_iota(I32, (nb, rows), 0)
    scm = jnp.where((blk + 1) * NSA_BLOCK <= qpos + 1, sc, NEG_INF)
    e = jnp.exp2(scm - jnp.max(scm, axis=0, keepdims=True))
    pc = e / jnp.sum(e, axis=0, keepdims=True)
    pc = pc * jnp.where(qpos + 1 >= NSA_BLOCK, 1.0, 0.0)
    imp = pc[:, 0:tq]
    for h in range(1, B_HEADS):
        imp = imp + pc[:, h * tq:(h + 1) * tq]
    qpos_t = pos0 + lax.broadcasted_iota(I32, (nb, tq), 1)
    blk_t = lax.broadcasted_iota(I32, (nb, tq), 0)
    cur = (qpos_t // NSA_BLOCK) == blk_t
    valid_t = (blk_t + 1) * NSA_BLOCK <= qpos_t + 1
    sel = jnp.where(cur, FORCE_SCORE, jnp.where(valid_t, imp, NEG_INF))
    blk_col = lax.broadcasted_iota(I32, (nb, 1), 0)
    rank = jnp.zeros((nb, tq), I32)
    for i in range(nb):
        row = sel[i:i + 1, :]
        rank = rank + jnp.where(row > sel, 1, jnp.where(row == sel, jnp.where(blk_col > i, 1, 0), 0))
    chosen_t = jnp.where((rank < NSA_TOPK) & (sel > NEG_INF / 2), 1.0, 0.0).astype(BF16)
    eye = jnp.where(lax.broadcasted_iota(I32, (tq, tq), 0) == lax.broadcasted_iota(I32, (tq, tq), 1), 1.0, 0.0)
    return _dot_nt(eye.astype(BF16), chosen_t).astype(BF16)


def _nsa_compressed_branch(qc, kcvc, pos0, tq, transposed_select=False):
    nb = kcvc.shape[0]
    rows = B_HEADS * tq
    lane = lax.broadcasted_iota(I32, kcvc.shape, 1)
    kc = jnp.where(lane < B_HD, kcvc, 0.0).astype(BF16)
    vc = jnp.where(lane >= B_HD, kcvc, 0.0).astype(BF16)
    sc = _dot_nt(qc, kc)
    qpos = pos0 + (lax.broadcasted_iota(I32, (rows, nb), 0) & (tq - 1))
    blk = lax.broadcasted_iota(I32, (rows, nb), 1)
    valid = (blk + 1) * NSA_BLOCK <= qpos + 1
    scm = jnp.where(valid, sc, NEG_INF)
    e = jnp.exp2(scm - jnp.max(scm, axis=-1, keepdims=True))
    pc = e / jnp.sum(e, axis=-1, keepdims=True)
    pc = pc * jnp.where(qpos + 1 >= NSA_BLOCK, 1.0, 0.0)
    o_c = _dot(pc.astype(BF16), vc)
    if transposed_select:
        return o_c, _select_blocks_transposed(qc, kc, pos0, tq)
    imp = jnp.sum(pc.reshape(B_HEADS, tq, nb), axis=0)
    qpos_t = pos0 + lax.broadcasted_iota(I32, (tq, nb), 0)
    blk_t = lax.broadcasted_iota(I32, (tq, nb), 1)
    cur = (qpos_t // NSA_BLOCK) == blk_t
    valid_t = (blk_t + 1) * NSA_BLOCK <= qpos_t + 1
    sel = jnp.where(cur, FORCE_SCORE, jnp.where(valid_t, imp, NEG_INF))
    blk_row = lax.broadcasted_iota(I32, (1, nb), 1)
    rank = jnp.zeros((tq, nb), I32)
    for i in range(nb):
        col = sel[:, i:i + 1]
        rank = rank + jnp.where(col > sel, 1, jnp.where(col == sel, jnp.where(blk_row > i, 1, 0), 0))
    chosen = (rank < NSA_TOPK) & (sel > NEG_INF / 2)
    return o_c, jnp.where(chosen, 1.0, 0.0).astype(BF16)


def _expand_blocks(chosen, first_tok, width):
    nb = chosen.shape[1]
    tok = first_tok + lax.broadcasted_iota(I32, (nb, width), 1)
    blk = lax.broadcasted_iota(I32, (nb, width), 0)
    e = jnp.where((tok // NSA_BLOCK) == blk, 1.0, 0.0).astype(BF16)
    return _dot(chosen, e)


def _nsa_combine(g, o_c, o_s, o_w, tq):
    gate = jax.nn.sigmoid(g)
    lane = lax.broadcasted_iota(I32, (tq, LANE), 1)
    comb = []
    for h in range(B_HEADS):
        sl = slice(h * tq, (h + 1) * tq)
        comb.append(gate[:, 3 * h:3 * h + 1] * o_c[sl] + gate[:, 3 * h + 1:3 * h + 2] * o_s[sl]
                    + gate[:, 3 * h + 2:3 * h + 3] * o_w[sl])
    groups = [jnp.where(lane < B_HD, pltpu.roll(comb[2 * j], B_HD, 1), comb[2 * j + 1]) for j in range(2)]
    return jnp.concatenate(groups, axis=1)


def _nsa_prompt_kernel(qbp_ref, qbrp_ref, kcvc_ref, kvs_ref, kvw_ref, g_ref, o_ref, *, tq, tk):
    qi = pl.program_id(1)
    pos0 = qi * tq
    rows = B_HEADS * tq
    qc = jnp.concatenate([qbp_ref[:, h * LANE:(h + 1) * LANE] for h in range(B_HEADS)], axis=0)
    o_c, chosen = _nsa_compressed_branch(qc, kcvc_ref[...], pos0, tq, transposed_select=(tq % LANE == 0))
    qr = jnp.concatenate([qbrp_ref[:, h * LANE:(h + 1) * LANE] for h in range(B_HEADS)], axis=0)

    def masked(s, ok):
        w = s.shape[1]
        return jnp.where(ok[None], s.reshape(B_HEADS, tq, w), NEG_INF).reshape(rows, w)

    def values(kv):
        lane = lax.broadcasted_iota(I32, kv.shape, 1)
        return jnp.where(lane < B_HD, jnp.ones_like(kv), kv)

    qpos_s = pos0 + lax.broadcasted_iota(I32, (tq, tk), 0)
    c_s = lax.broadcasted_iota(I32, (tq, tk), 1)

    def sel_step(j, carry):
        m, acc = carry
        start = pl.multiple_of(j * tk, tk)
        kv = kvs_ref[pl.ds(start, tk), :]
        ok = (_expand_blocks(chosen, start, tk) > 0.5) & (start + c_s <= qpos_s)
        s = masked(_dot_nt(qr, kv), ok)
        m_new = jnp.maximum(m, jnp.max(s, axis=-1, keepdims=True))
        p = jnp.exp2(s - m_new)
        acc = jnp.exp2(m - m_new) * acc + _dot(p.astype(BF16), values(kv))
        return m_new, acc

    n_chunks = (pos0 + tq - 1) // tk + 1
    init = (jnp.full((rows, 1), NEG_INF, F32), jnp.zeros((rows, LANE), F32))
    _, acc = lax.fori_loop(0, n_chunks, sel_step, init)
    def normalised(a):
        lane = lax.broadcasted_iota(I32, a.shape, 1)
        return a / jnp.where(lane < B_HD, 1.0, pltpu.roll(a, B_HD, 1))

    o_s = normalised(acc)

    span = WINDOW + tq
    start = pl.multiple_of(jnp.maximum(pos0 - WINDOW, 0), tq)
    kv = kvw_ref[pl.ds(start, span), :]
    qpos_w = pos0 + lax.broadcasted_iota(I32, (tq, span), 0)
    tok_w = start + lax.broadcasted_iota(I32, (tq, span), 1)
    s = masked(_dot_nt(qr, kv), (tok_w <= qpos_w) & (tok_w > qpos_w - WINDOW))
    p = jnp.exp2(s - jnp.max(s, axis=-1, keepdims=True))
    acc = _dot(p.astype(BF16), values(kv))
    o_w = normalised(acc)

    o_ref[...] = _nsa_combine(g_ref[...], o_c, o_s, o_w, tq).astype(BF16)


def _nsa_prompt(qbp, qbrp, kcvc, kvsb, kvwb, g, nbatch, seq, tq, tk):
    nq = seq // tq
    nb = seq // NSA_BLOCK
    assert tk % tq == 0 and seq % tk == 0 and seq >= WINDOW + tq and WINDOW % tq == 0
    tile = lambda w: pl.BlockSpec((tq, w), lambda b, i: (b * nq + i, 0))
    return pl.pallas_call(
        functools.partial(_nsa_prompt_kernel, tq=tq, tk=tk),
        grid=(nbatch, nq),
        in_specs=[tile(4 * LANE), tile(4 * LANE), pl.BlockSpec((None, nb, LANE), lambda b, i: (b, 0, 0)),
                  pl.BlockSpec((seq, LANE), lambda b, i: (b, 0)), pl.BlockSpec((seq, LANE), lambda b, i: (b, 0)),
                  tile(LANE)],
        out_specs=tile(B_W),
        out_shape=jax.ShapeDtypeStruct((nbatch * seq, B_W), BF16),
        compiler_params=_params("parallel", "arbitrary"),
        name="nsa_prompt",
    )(qbp, qbrp, kcvc, kvsb, kvwb, g)


def _nsa_sample_body(qbp_ref, qbrp_ref, g_ref, cn_ref, sn_ref, wn_ref, win_ref, pe_ref, w_ref, cmp_pages, slc_pages,
                     o_ref, msum_ref, *, n_new, past, nbp):
    n_pages = len(cmp_pages)
    tq = TOK_PAD
    rows = B_HEADS * tq
    page_rows = cmp_pages[0].shape[0]
    bpp = page_rows // NSA_BLOCK
    nb_past = n_pages * bpp

    for p, pg in enumerate(cmp_pages):
        msum_ref[p * bpp:(p + 1) * bpp, :] = jnp.sum(pg[...].reshape(bpp, NSA_BLOCK, LANE), axis=1)
    msum_ref[nb_past:nb_past + 1, :] = jnp.sum(cn_ref[...], axis=0, keepdims=True)
    msum_ref[nb_past + 1:nbp, :] = jnp.zeros((nbp - nb_past - 1, LANE), F32)
    mblk = msum_ref[...] * (1.0 / NSA_BLOCK) + jnp.mean(pe_ref[...], axis=0, keepdims=True)
    kcvc = jnp.dot(mblk, w_ref[...], preferred_element_type=F32, precision=lax.Precision.HIGHEST)

    qc = jnp.concatenate([qbp_ref[:, h * LANE:(h + 1) * LANE] for h in range(B_HEADS)], axis=0)
    o_c, chosen = _nsa_compressed_branch(qc, kcvc, past, tq)
    qr = jnp.concatenate([qbrp_ref[:, h * LANE:(h + 1) * LANE] for h in range(B_HEADS)], axis=0)

    def masked(s, ok):
        w = s.shape[1]
        return jnp.where(ok[None], s.reshape(B_HEADS, tq, w), NEG_INF).reshape(rows, w)

    tok_n = lax.broadcasted_iota(I32, (tq, NEW_ROWS), 0)
    j_n = lax.broadcasted_iota(I32, (tq, NEW_ROWS), 1)
    new_ok = (j_n <= tok_n) & (j_n < n_new)

    def attend(score_tiles, value_tiles):
        m = None
        for s in score_tiles:
            mx = jnp.max(s, axis=-1, keepdims=True)
            m = mx if m is None else jnp.maximum(m, mx)
        l = jnp.zeros((rows, 1), F32)
        acc = jnp.zeros((rows, LANE), F32)
        for s, v in zip(score_tiles, value_tiles):
            p = jnp.exp2(s - m)
            l = l + jnp.sum(p, axis=-1, keepdims=True)
            acc = acc + _dot(p.astype(BF16), v)
        return acc / l

    tiles, vals = [], []
    for p, pg in enumerate(slc_pages):
        kv = pg[...].astype(BF16)
        ok = _expand_blocks(chosen, p * page_rows, page_rows) > 0.5
        tiles.append(masked(_dot_nt(qr, kv), ok))
        vals.append(kv)
    kv = sn_ref[...].astype(BF16)
    ok = (_expand_blocks(chosen, past, NEW_ROWS) > 0.5) & new_ok
    tiles.append(masked(_dot_nt(qr, kv), ok))
    vals.append(kv)
    o_s = attend(tiles, vals)

    wb = win_ref.shape[0]
    kv = win_ref[...].astype(BF16)
    r_w = lax.broadcasted_iota(I32, (tq, wb), 1)
    t_w = lax.broadcasted_iota(I32, (tq, wb), 0)
    tiles = [masked(_dot_nt(qr, kv), r_w > t_w + (wb - WINDOW))]
    vals = [kv]
    kv = wn_ref[...].astype(BF16)
    tiles.append(masked(_dot_nt(qr, kv), new_ok))
    vals.append(kv)
    o_w = attend(tiles, vals)

    o_ref[...] = _nsa_combine(g_ref[...], o_c, o_s, o_w, tq).astype(BF16)


N_DIFF_SEQ_IN = 5
N_NSA_SEQ_IN = 9


def _sample_mixers_kernel(pt_ref, *refs, n_pages, n_new, past, nbp, lam_init):
    del pt_ref
    diff_in = refs[:N_DIFF_SEQ_IN]
    nsa_in = refs[N_DIFF_SEQ_IN:N_DIFF_SEQ_IN + N_NSA_SEQ_IN]
    pages = refs[N_DIFF_SEQ_IN + N_NSA_SEQ_IN:]
    k_pages, v_pages, cmp_pages, slc_pages = (pages[i * n_pages:(i + 1) * n_pages] for i in range(4))
    oa_ref, ob_ref, msum_ref = pages[4 * n_pages:]
    _nsa_sample_body(*nsa_in, cmp_pages, slc_pages, ob_ref, msum_ref, n_new=n_new, past=past, nbp=nbp)
    _diff_sample_body(*diff_in, k_pages, v_pages, oa_ref, n_new=n_new, lam_init=lam_init)


def _sample_mixers(pt_flat, lw, subw, q8, kn, vn, qbp8, qbrp8, g8, cn, sn, wn, state_win, pe_cat, w_bd, cache_k,
                   cache_v, cache_cmp, cache_slc, layer, lam_init, n_pages, n_new, past):
    nseq = q8.shape[0]
    new_rows = n_new * A_HEADS
    assert new_rows % 8 == 0
    wb = state_win.shape[2]
    nb = -(-(past + n_new) // NSA_BLOCK)
    nbp = -(-nb // 16) * 16

    def page_specs(cache):
        rows = cache.shape[2]
        return [pl.BlockSpec((None, None, rows, LANE), lambda b, pt, p=p: (layer, pt[b * n_pages + p], 0, 0))
                for p in range(n_pages)]

    seq_spec = lambda r, w: pl.BlockSpec((None, r, w), lambda b, pt: (b, 0, 0))
    const_spec = lambda r, w: pl.BlockSpec((r, w), lambda b, pt: (0, 0))
    caches = (cache_k, cache_v, cache_cmp, cache_slc)
    grid_spec = pltpu.PrefetchScalarGridSpec(
        num_scalar_prefetch=1,
        grid=(nseq,),
        in_specs=[const_spec(4, A_HD), const_spec(1, LANE), seq_spec(TOK_PAD, A_W), seq_spec(new_rows, LANE),
                  seq_spec(new_rows, LANE),
                  seq_spec(TOK_PAD, 4 * LANE), seq_spec(TOK_PAD, 4 * LANE), seq_spec(TOK_PAD, LANE),
                  seq_spec(NEW_ROWS, LANE), seq_spec(NEW_ROWS, LANE), seq_spec(NEW_ROWS, LANE),
                  pl.BlockSpec((None, None, wb, LANE), lambda b, pt: (layer, b, 0, 0)),
                  const_spec(NSA_BLOCK, LANE), const_spec(LANE, LANE)]
        + [s for c in caches for s in page_specs(c)],
        out_specs=[seq_spec(TOK_PAD, A_W), seq_spec(TOK_PAD, B_W)],
        scratch_shapes=[pltpu.VMEM((nbp, LANE), F32)],
    )
    return pl.pallas_call(
        functools.partial(_sample_mixers_kernel, n_pages=n_pages, n_new=n_new, past=past, nbp=nbp,
                          lam_init=lam_init),
        grid_spec=grid_spec,
        out_shape=[jax.ShapeDtypeStruct((nseq, TOK_PAD, A_W), BF16), jax.ShapeDtypeStruct((nseq, TOK_PAD, B_W), BF16)],
        compiler_params=_params("parallel"),
        name="sample_mixers",
    )(pt_flat, lw, subw, q8, kn, vn, qbp8, qbrp8, g8, cn, sn, wn, state_win, pe_cat, w_bd,
      *[c for c in caches for _ in range(n_pages)])


def _conv_prompt_kernel(c_ref, halo_ref, b_ref, w_ref, o_ref):
    i = pl.program_id(1)
    c = c_ref[...]
    rows = c.shape[0]
    halo = jnp.where(i > 0, halo_ref[...], 0.0)
    r = lax.broadcasted_iota(I32, c.shape, 0)
    c1 = jnp.where(r == 0, halo[7:8], pltpu.roll(c, 1, 0))
    c2 = jnp.where(r == 0, halo[6:7], jnp.where(r == 1, halo[7:8], pltpu.roll(c, 2, 0)))
    del rows
    z = w_ref[0:1] * c2 + w_ref[1:2] * c1 + w_ref[2:3] * c
    o_ref[...] = (b_ref[...] * z).astype(BF16)


def _conv_prompt(cin, bgate, conv_w, nbatch, seq, tm):
    nt = seq // tm
    hb = tm // 8
    tile = pl.BlockSpec((tm, C_W), lambda b, i: (b * nt + i, 0))
    return pl.pallas_call(
        _conv_prompt_kernel,
        grid=(nbatch, nt),
        in_specs=[tile, pl.BlockSpec((8, C_W), lambda b, i: (jnp.maximum((b * nt + i) * hb - 1, 0), 0)), tile,
                  pl.BlockSpec((CONV_W, C_W), lambda b, i: (0, 0))],
        out_specs=tile,
        out_shape=jax.ShapeDtypeStruct((nbatch * seq, C_W), BF16),
        compiler_params=_params("parallel", "parallel"),
        name="conv_prompt",
    )(cin, cin, bgate, conv_w)


def _conv_sample_kernel(c0_ref, c1_ref, c2_ref, b_ref, w_ref, o_ref):
    z = w_ref[0:1] * c0_ref[...] + w_ref[1:2] * c1_ref[...] + w_ref[2:3] * c2_ref[...]
    o_ref[...] = (b_ref[...] * z).astype(BF16)


def _conv_sample(c0, c1, c2, bgate, conv_w):
    return pl.pallas_call(
        _conv_sample_kernel,
        out_shape=jax.ShapeDtypeStruct(c0.shape, BF16),
        name="conv_sample",
    )(c0, c1, c2, bgate, conv_w)


def _mix_out_kernel(x_ref, oa_ref, ob_ref, oc_ref, wout_ref, nw_ref, wq_ref, x_out_ref, q_out_ref):
    y = (_dot(oa_ref[...], wout_ref[0:A_W]) + _dot(ob_ref[...], wout_ref[A_W:A_W + B_W])
         + _dot(oc_ref[...], wout_ref[A_W + B_W:A_W + B_W + C_W]))
    x = x_ref[...] + _rms(y, nw_ref[NORM_MIX_POST:NORM_MIX_POST + 1])
    x_out_ref[...] = x
    h = _rms(x, nw_ref[NORM_X_PRE:NORM_X_PRE + 1]).astype(BF16)
    q_out_ref[...] = (_dot(h, wq_ref[...]) * Q_SCALE).astype(BF16)


def _mix_out(x, mixed, w_out, nw8, w_q, tm):
    n, d = x.shape
    row = lambda w: pl.BlockSpec((tm, w), lambda i: (i, 0))
    full = lambda a: pl.BlockSpec(a.shape, lambda i: (0, 0))
    return pl.pallas_call(
        _mix_out_kernel,
        grid=(n // tm,),
        in_specs=[row(d)] + [row(a.shape[1]) for a in mixed] + [full(w_out), full(nw8), full(w_q)],
        out_specs=[row(d), row(X_W)],
        out_shape=[jax.ShapeDtypeStruct((n, d), F32), jax.ShapeDtypeStruct((n, X_W), BF16)],
        compiler_params=_params("parallel"),
        name="mix_out",
    )(x, *mixed, w_out, nw8, w_q)


def _cross_attn_kernel(q_ref, mkt_ref, mvt_ref, o_ref):
    for gi in range(q_ref.shape[0]):
        q = q_ref[gi]
        ts = q.shape[0]
        lane = lax.broadcasted_iota(I32, q.shape, 1)
        zero = jnp.zeros_like(q)
        qs = jnp.concatenate(
            [jnp.where((lane >= h * X_HD) & (lane < (h + 1) * X_HD), q, zero) for h in range(X_HEADS)], axis=0)
        s = _dot(qs, mkt_ref[gi].astype(BF16))
        e = jnp.exp2(s - jnp.max(s, axis=-1, keepdims=True))
        p = e / jnp.sum(e, axis=-1, keepdims=True)
        o = _dot_nt(p.astype(BF16), mvt_ref[gi].astype(BF16))
        out = jnp.zeros((ts, X_W), F32)
        for h in range(X_HEADS):
            out = out + jnp.where((lane >= h * X_HD) & (lane < (h + 1) * X_HD), o[h * ts:(h + 1) * ts], 0.0)
        o_ref[gi] = out.astype(BF16)


def _cross_attn(q, mkt, mvt, ts, v_block=0, b_off=0, group=1):
    nbatch, seq, _ = q.shape
    n_mem = mkt.shape[2]
    assert nbatch % group == 0 and b_off % group == 0
    g_off = b_off // group
    return pl.pallas_call(
        _cross_attn_kernel,
        grid=(nbatch // group, seq // ts),
        in_specs=[pl.BlockSpec((group, ts, X_W), lambda b, i: (b, i, 0)),
                  pl.BlockSpec((group, X_W, n_mem), lambda b, i: (g_off + b, 0, 0)),
                  pl.BlockSpec((group, X_W, n_mem), lambda b, i: (g_off + b, v_block, 0))],
        out_specs=pl.BlockSpec((group, ts, X_W), lambda b, i: (b, i, 0)),
        out_shape=jax.ShapeDtypeStruct((nbatch, seq, X_W), BF16),
        compiler_params=_params("parallel", "parallel"),
        name="cross_attn",
    )(q, mkt, mvt)


def _mem_kv_kernel(m_ref, nw_ref, wt_ref, o_ref):
    h = _rms(m_ref[...], nw_ref[...]).astype(BF16)
    o_ref[...] = _dot_nt(wt_ref[...], h)


def _mem_kv(mem, nw, w_kv_t):
    nbatch, n_mem, d = mem.shape
    return pl.pallas_call(
        _mem_kv_kernel,
        grid=(nbatch,),
        in_specs=[pl.BlockSpec((None, n_mem, d), lambda b: (b, 0, 0)), pl.BlockSpec((1, d), lambda b: (0, 0)),
                  pl.BlockSpec(w_kv_t.shape, lambda b: (0, 0))],
        out_specs=pl.BlockSpec((None, 2 * X_W, n_mem), lambda b: (b, 0, 0)),
        out_shape=jax.ShapeDtypeStruct((nbatch, 2 * X_W, n_mem), F32),
        compiler_params=_params("parallel"),
        name="mem_kv",
    )(mem, nw, w_kv_t)


def _ffn_kernel(x_ref, o_ref, wo_ref, nw_ref, wg_ref, wu_ref, wd_ref, y_ref, x2_ref, h_ref, acc_ref):
    j = pl.program_id(1)

    @pl.when(j == 0)
    def _():
        x2 = x_ref[...] + _rms(_dot(o_ref[...], wo_ref[...]), nw_ref[NORM_X_POST:NORM_X_POST + 1])
        x2_ref[...] = x2
        h_ref[...] = _rms(x2, nw_ref[NORM_FFN_PRE:NORM_FFN_PRE + 1]).astype(BF16)
        acc_ref[...] = jnp.zeros_like(acc_ref)

    h = h_ref[...]
    g = _dot(h, wg_ref[...])
    u = _dot(h, wu_ref[...])
    a = (g * jax.nn.sigmoid(g)) * u
    acc_ref[...] += _dot(a.astype(BF16), wd_ref[...])

    @pl.when(j == pl.num_programs(1) - 1)
    def _():
        y_ref[...] = x2_ref[...] + _rms(acc_ref[...], nw_ref[NORM_FFN_POST:NORM_FFN_POST + 1])


def _ffn(x, o, w_o, nw8, w_g, w_u, w_d, tm, tf):
    n, d = x.shape
    dff = w_g.shape[1]
    row = lambda w: pl.BlockSpec((tm, w), lambda i, j: (i, 0))
    return pl.pallas_call(
        _ffn_kernel,
        grid=(n // tm, dff // tf),
        in_specs=[row(d), row(X_W), pl.BlockSpec(w_o.shape, lambda i, j: (0, 0)),
                  pl.BlockSpec(nw8.shape, lambda i, j: (0, 0)),
                  pl.BlockSpec((d, tf), lambda i, j: (0, j)), pl.BlockSpec((d, tf), lambda i, j: (0, j)),
                  pl.BlockSpec((tf, d), lambda i, j: (j, 0))],
        out_specs=row(d),
        out_shape=jax.ShapeDtypeStruct((n, d), F32),
        scratch_shapes=[pltpu.VMEM((tm, d), F32), pltpu.VMEM((tm, d), BF16), pltpu.VMEM((tm, d), F32)],
        compiler_params=_params("parallel", "arbitrary"),
        name="ffn",
    )(x, o, w_o, nw8, w_g, w_u, w_d)


def _rope_tables(pos):
    inv = ROPE_THETA ** (-jnp.arange(ROPE_HALF, dtype=F32) / ROPE_HALF)
    ang = pos.astype(F32)[:, None] * inv[None, :]
    cos, sin = jnp.cos(ang), jnp.sin(ang)
    n = pos.shape[0]
    rest = A_HD - 2 * ROPE_HALF
    z8 = jnp.zeros((n, ROPE_HALF), F32)
    c = jnp.concatenate([cos, cos, jnp.ones((n, rest), F32)], axis=1)
    s1 = jnp.concatenate([-sin, z8, jnp.zeros((n, rest), F32)], axis=1)
    s2 = jnp.concatenate([z8, sin, jnp.zeros((n, rest), F32)], axis=1)
    return jnp.concatenate([c, c, s1, s1, s2, s2], axis=1)


def _pad_rows(t, rows):
    return jnp.pad(t, ((0, 0), (0, rows - t.shape[1]), (0, 0)))


def kernel(x_prompt, x_sample, cache_diff_k, cache_diff_v, cache_nsa_cmp, cache_nsa_slc, state_nsa_win, state_conv,
           cache_mem_k, cache_mem_v, page_table, mem_prompt, norm_w, w_in, w_out, diff_lambda, diff_subln, nsa_pe,
           nsa_w_cmp, conv_w, w_q_mem, w_kv_mem, w_o_mem, w_ffn_up, w_ffn_down):
    depth = w_in.shape[0]
    nbp, seq, d = x_prompt.shape
    nbs, sseq, _ = x_sample.shape
    n_p, n_s = nbp * seq, nbs * sseq
    n_pool, page_rows = cache_diff_k.shape[1], cache_diff_k.shape[2]
    n_pages = page_table.shape[1]
    past = n_pages * page_rows
    n_mem = mem_prompt.shape[1]
    dff = w_ffn_down.shape[1]
    assert seq % 256 == 0 and seq >= WINDOW and sseq <= TOK_PAD and d == 1024

    tm_p = _pick_tile(seq, 512)
    tm_s = _pick_tile(n_s, 512)
    tf = _pick_tile(dff, 1408, 128)

    split = 2176
    w_in_p = jnp.concatenate(
        [w_in[:, :, :split], w_in[:, :, split + 3 * B_HEADS:], w_in[:, :, split:split + 3 * B_HEADS],
         jnp.zeros((depth, d, IN_COLS_PADDED - w_in.shape[2]), w_in.dtype)], axis=2).astype(BF16)
    w_out_b = w_out.astype(BF16)
    w_q_b = w_q_mem.astype(BF16)
    w_kv_t = jnp.swapaxes(w_kv_mem, 1, 2).astype(BF16)
    mkt_s, mvt_s = (jnp.transpose(t, (0, 1, 3, 4, 2)).reshape(depth * nbs, X_W, n_mem)
                    for t in (cache_mem_k, cache_mem_v))
    w_o_b = w_o_mem.astype(BF16)
    w_g_b = w_ffn_up[:, :, :dff].astype(BF16)
    w_u_b = w_ffn_up[:, :, dff:].astype(BF16)
    w_d_b = w_ffn_down.astype(BF16)
    nw8 = jnp.pad(norm_w, ((0, 0), (0, 1), (0, 0)))
    pe_cat = jnp.concatenate([nsa_pe[:, 0], nsa_pe[:, 1]], axis=-1)
    zb = jnp.zeros((depth, B_HD, B_HD), F32)
    w_bd = jnp.concatenate([jnp.concatenate([nsa_w_cmp[:, 0], zb], axis=2),
                            jnp.concatenate([zb, nsa_w_cmp[:, 1]], axis=2)], axis=1)
    subw = diff_subln.reshape(depth, 1, 2 * A_HD)

    rope_p = _rope_tables(jnp.arange(seq, dtype=I32))
    rope_s = _rope_tables(jnp.tile(past + jnp.arange(sseq, dtype=I32), nbs))
    pt_flat = page_table.reshape(-1).astype(I32)
    ck = cache_diff_k.reshape(depth, n_pool, page_rows * A_HEADS, 2 * A_HD)
    cv = cache_diff_v.reshape(depth, n_pool, page_rows * A_HEADS, 2 * A_HD)

    xp = x_prompt.reshape(n_p, d)
    xs = x_sample.reshape(n_s, d)
    outs = [[] for _ in range(14)]

    def sample3(t, rows):
        return _pad_rows(t.reshape(nbs, sseq, t.shape[1]), rows)

    for l in range(depth):
        lam_init = 0.8 - 0.6 * math.exp(-0.3 * l)
        nw_pre = norm_w[l, NORM_MIX_PRE:NORM_MIX_PRE + 1]

        (qa, ka, kab, va, vab, qbp, qbrp, kvc, kvs, kvsb, kvw, kvwb, g, cin, bgate) = _proj_in(
            xp, nw_pre, w_in_p[l], rope_p, tm_p)
        oa_p = _diff_prompt(qa, kab, vab, diff_lambda[l], subw[l], lam_init, nbp, seq, 512, 512)
        kcvc = _nsa_compress(kvc, pe_cat[l], w_bd[l], nbp, seq)
        ob_p = _nsa_prompt(qbp, qbrp, kcvc, kvsb, kvwb, g, nbp, seq, 256, 512)
        oc_p = _conv_prompt(cin, bgate, conv_w[l], nbp, seq, _pick_tile(seq, 512))
        xp, qx_p = _mix_out(xp, (oa_p, ob_p, oc_p), w_out_b[l], nw8[l], w_q_b[l], tm_p)
        mkv_t = _mem_kv(mem_prompt, norm_w[l, NORM_MEM:NORM_MEM + 1], w_kv_t[l])
        ox_p = _cross_attn(qx_p.reshape(nbp, seq, X_W), mkv_t, mkv_t, _pick_tile(seq, 512), v_block=1)
        xp = _ffn(xp, ox_p.reshape(n_p, X_W), w_o_b[l], nw8[l], w_g_b[l], w_u_b[l], w_d_b[l], tm_p, tf)
        mk_p, mv_p = (jnp.transpose(t.reshape(nbp, X_HEADS, X_HD, n_mem), (0, 3, 1, 2))
                      for t in (mkv_t[:, :X_W], mkv_t[:, X_W:]))

        (qa_s, ka_s, _, va_s, _, qbp_s, qbrp_s, kvc_s, kvs_s, _, kvw_s, _, g_s, cin_s, bgate_s) = _proj_in(
            xs, nw_pre, w_in_p[l], rope_s, tm_s)
        new_heads = lambda t: t.reshape(nbs, sseq * A_HEADS, 2 * A_HD)
        oa_s, ob_s = _sample_mixers(
            pt_flat, diff_lambda[l], subw[l], sample3(qa_s, TOK_PAD), new_heads(ka_s), new_heads(va_s),
            sample3(qbp_s, TOK_PAD), sample3(qbrp_s, TOK_PAD), sample3(g_s, TOK_PAD), sample3(kvc_s, NEW_ROWS),
            sample3(kvs_s, NEW_ROWS), sample3(kvw_s, NEW_ROWS), state_nsa_win, pe_cat[l], w_bd[l], ck, cv,
            cache_nsa_cmp, cache_nsa_slc, l, lam_init, n_pages, sseq, past)
        c_all = jnp.concatenate([state_conv[l], cin_s.reshape(nbs, sseq, C_W)], axis=1)
        oc_s = _conv_sample(*[c_all[:, j:j + sseq].reshape(n_s, C_W) for j in range(CONV_W)], bgate_s, conv_w[l])
        xs, qx_s = _mix_out(xs, (oa_s[:, :sseq].reshape(n_s, A_W), ob_s[:, :sseq].reshape(n_s, B_W), oc_s),
                            w_out_b[l], nw8[l], w_q_b[l], tm_s)
        ox_s = _cross_attn(_pad_rows(qx_s.reshape(nbs, sseq, X_W), TOK_PAD), mkt_s, mvt_s, TOK_PAD, b_off=l * nbs,
                           group=math.gcd(nbs, 8))
        xs = _ffn(xs, ox_s[:, :sseq].reshape(n_s, X_W), w_o_b[l], nw8[l], w_g_b[l], w_u_b[l], w_d_b[l], tm_s, tf)

        win_p = kvw.reshape(nbp, seq, 2 * B_HD)[:, seq - WINDOW:]
        cin_p = cin.reshape(nbp, seq, C_W)
        win_s = kvw_s.reshape(nbs, sseq, 2 * B_HD)
        layer_out = (
            ka.reshape(nbp, seq, A_HEADS, 2 * A_HD), va.reshape(nbp, seq, A_HEADS, 2 * A_HD),
            kvc.reshape(nbp, seq, 2 * B_HD), kvs.reshape(nbp, seq, 2 * B_HD), win_p,
            cin_p[:, seq - (CONV_W - 1):], mk_p, mv_p,
            ka_s.reshape(nbs, sseq, A_HEADS, 2 * A_HD), va_s.reshape(nbs, sseq, A_HEADS, 2 * A_HD),
            kvc_s.reshape(nbs, sseq, 2 * B_HD), kvs_s.reshape(nbs, sseq, 2 * B_HD), win_s, c_all[:, sseq:])
        for acc, t in zip(outs, layer_out):
            acc.append(t)

    stacked = [jnp.stack(t) for t in outs]
    stacked[12] = jnp.concatenate([state_nsa_win, stacked[12]], axis=2)[:, :, sseq:]
    return (xp.reshape(nbp, seq, d), xs.reshape(nbs, sseq, d), *stacked)
```
